```python
import jax, jax.numpy as jnp
from jax import lax
import numpy as np

D_MODEL = 1024
BATCH = 8
SEQ = 8192
DEPTH = 2
DEC_BATCH = 16
DEC_SEQ = 16
PAST_LEN = 1024

CHUNK = 64
N_A_LAYERS = DEPTH // 2
N_B_LAYERS = DEPTH - N_A_LAYERS
RET_HEADS = 4
RET_DK = D_MODEL // RET_HEADS
RET_DV = 2 * RET_DK
RET_QK_W = RET_HEADS * RET_DK
RET_VW = RET_HEADS * RET_DV
RET_IN = 2 * RET_QK_W + 2 * RET_VW
ROPE_BASE = 10000.0
FOX_HEADS = 8
FOX_KV_HEADS = 4
FOX_GROUP = FOX_HEADS // FOX_KV_HEADS
FOX_DH = 128
FOX_W = FOX_HEADS * FOX_DH
FOX_IN = 2 * FOX_W
KV_W = FOX_KV_HEADS * FOX_DH
KV_OUT = 2 * KV_W + FOX_HEADS
Q_BLOCK = 128
EPS = 1e-6

kernel_name = 'yoco_retention_fox_streaming_step'

F32 = jnp.float32


def rms_norm(x, g):
    xf = x.astype(F32)
    y = xf * lax.rsqrt(jnp.mean(xf * xf, axis=-1, keepdims=True) + EPS)
    return (y * g.astype(F32)).astype(x.dtype)


def ada_mod(c, w, b):
    return jax.nn.silu(c) @ w + b


def modulate(x, g, shift, scale):
    return rms_norm(x, g) * (1 + scale[:, None, :]) + shift[:, None, :]


def rotary(x, pos):
    half = x.shape[-1] // 2
    inv = jnp.power(ROPE_BASE, -jnp.arange(half, dtype=F32) / half)
    ang = pos.astype(F32)[:, None] * inv[None, :]
    cos = jnp.cos(ang)[None, :, None, :]
    sin = jnp.sin(ang)[None, :, None, :]
    xf = x.astype(F32)
    x1, x2 = xf[..., :half], xf[..., half:]
    return jnp.concatenate([x1 * cos - x2 * sin, x2 * cos + x1 * sin], axis=-1).astype(x.dtype)


def head_norm(o, g):
    of = o.astype(F32)
    mu = jnp.mean(of, axis=-1, keepdims=True)
    var = jnp.mean(jnp.square(of - mu), axis=-1, keepdims=True)
    return ((of - mu) * lax.rsqrt(var + EPS) * g.astype(F32)).astype(o.dtype)


def ret_log_gamma():
    return jnp.log1p(-jnp.exp2(-5.0 - jnp.arange(RET_HEADS, dtype=F32)))


def retention_scan(q, k, v, r0):
    b, L = q.shape[0], q.shape[1]
    cs = min(CHUNK, L)
    n = L // cs
    lg = ret_log_gamma()
    idx = jnp.arange(cs, dtype=F32)
    diff = idx[:, None] - idx[None, :]
    intra = jnp.where(diff >= 0, jnp.exp(jnp.maximum(diff, 0.0)[None] * lg[:, None, None]), 0.0)
    q_dec = jnp.exp((idx + 1.0)[:, None] * lg[None, :])[None, :, :, None]
    k_dec = jnp.exp((cs - 1.0 - idx)[:, None] * lg[None, :])[None, :, :, None]
    c_dec = jnp.exp(cs * lg)[None, :, None, None]

    def to_chunks(t):
        return t.reshape(b, n, cs, *t.shape[2:]).swapaxes(0, 1)

    def step(r, qkv):
        qc, kc, vc = qkv
        qf, kf, vf = qc.astype(F32), kc.astype(F32), vc.astype(F32)
        s = jnp.einsum('bihd,bjhd->bhij', qf, kf) * intra[None]
        o = jnp.einsum('bhij,bjhe->bihe', s, vf)
        o = o + jnp.einsum('bihd,bhde->bihe', qf, r) * q_dec
        r = r * c_dec + jnp.einsum('bjhd,bjhe->bhde', kf * k_dec, vf)
        return r, o.astype(vc.dtype)

    r, o = lax.scan(step, r0.astype(F32), (to_chunks(q), to_chunks(k), to_chunks(v)))
    o = o.swapaxes(0, 1).reshape(b, L, RET_HEADS, RET_DV)
    return o, r.astype(r0.dtype)


def retention_layer(x, c, pos, r0, norm_g, ada_w, ada_b, w_in, gn_g, w_out):
    shift, scale, gate = jnp.split(ada_mod(c, ada_w, ada_b), 3, axis=-1)
    h = modulate(x, norm_g, shift, scale)
    b, L = x.shape[0], x.shape[1]
    q, k, v, g = jnp.split(h @ w_in, [RET_QK_W, 2 * RET_QK_W, 2 * RET_QK_W + RET_VW], axis=-1)
    q = rotary(q.reshape(b, L, RET_HEADS, RET_DK), pos)
    k = rotary(k.reshape(b, L, RET_HEADS, RET_DK), pos) * (RET_DK ** -0.5)
    v = v.reshape(b, L, RET_HEADS, RET_DV)
    o, r = retention_scan(q, k, v, r0)
    o = head_norm(o, gn_g.reshape(RET_HEADS, RET_DV)).reshape(b, L, RET_VW) * jax.nn.silu(g)
    return x + gate[:, None, :] * (o @ w_out), r


def shared_kv(x, c, kv_norm_g, kv_ada_w, kv_ada_b, kv_w, kv_fb, k_norm_g):
    shift, scale = jnp.split(ada_mod(c, kv_ada_w, kv_ada_b), 2, axis=-1)
    h = modulate(x, kv_norm_g, shift, scale)
    b, L = x.shape[0], x.shape[1]
    k, v, f = jnp.split(h @ kv_w, [KV_W, 2 * KV_W], axis=-1)
    k = rms_norm(k.reshape(b, L, FOX_KV_HEADS, FOX_DH), k_norm_g)
    v = v.reshape(b, L, FOX_KV_HEADS, FOX_DH)
    logf = jax.nn.log_sigmoid((f + kv_fb).astype(F32))
    return k, v, logf


def fox_attend(q, k, v, fq, fk, qpos, kpos):
    b, nq = q.shape[0], q.shape[1]
    s = jnp.einsum('bqkgd,bskd->bkgqs', q.astype(F32), k.astype(F32)) * (FOX_DH ** -0.5)
    fq_ = fq.reshape(b, nq, FOX_KV_HEADS, FOX_GROUP).transpose(0, 2, 3, 1)[..., None]
    s = s + fq_ - fk[:, :, :, None, :]
    s = jnp.where(kpos[None, :] <= qpos[:, None], s, -jnp.inf)
    p = jax.nn.softmax(s, axis=-1)
    o = jnp.einsum('bkgqs,bskd->bqkgd', p, v.astype(F32))
    return o.reshape(b, nq, FOX_HEADS, FOX_DH).astype(v.dtype)


def fox_layer(x, c, k, v, F, F_q, qpos, kpos, norm_g, ada_w, ada_b, w_in, q_norm_g, w_out):
    shift, scale, gate = jnp.split(ada_mod(c, ada_w, ada_b), 3, axis=-1)
    h = modulate(x, norm_g, shift, scale)
    b, L = x.shape[0], x.shape[1]
    q, g = jnp.split(h @ w_in, 2, axis=-1)
    q = rms_norm(q.reshape(b, L, FOX_HEADS, FOX_DH), q_norm_g).reshape(b, L, FOX_KV_HEADS, FOX_GROUP, FOX_DH)
    S = k.shape[1]
    fk = F.reshape(b, S, FOX_KV_HEADS, FOX_GROUP).transpose(0, 2, 3, 1)
    if L <= Q_BLOCK:
        o = fox_attend(q, k, v, F_q, fk, qpos, kpos)
    else:
        nb = L // Q_BLOCK
        qb = q.reshape(b, nb, Q_BLOCK, FOX_KV_HEADS, FOX_GROUP, FOX_DH).swapaxes(0, 1)
        fqb = F_q.reshape(b, nb, Q_BLOCK, FOX_HEADS).swapaxes(0, 1)
        pb = qpos.reshape(nb, Q_BLOCK)
        ob = lax.map(lambda a: fox_attend(a[0], k, v, a[1], fk, a[2], kpos), (qb, fqb, pb))
        o = ob.swapaxes(0, 1).reshape(b, L, FOX_HEADS, FOX_DH)
    o = o.reshape(b, L, FOX_W) * jax.nn.silu(g)
    return x + gate[:, None, :] * (o @ w_out)


def trunk(x, c, pos0, r0, k_past, v_past, logf_past,
          a_norm_g, a_ada_w, a_ada_b, a_w_in, a_gn_g, a_w_out,
          kv_norm_g, kv_ada_w, kv_ada_b, kv_w, kv_fb, k_norm_g,
          b_norm_g, b_ada_w, b_ada_b, b_w_in, b_q_norm_g, b_w_out):
    L = x.shape[1]
    qpos = pos0 + jnp.arange(L)
    r_out = []
    k_new = v_new = logf_new = None
    k_all = v_all = F = F_q = kpos = None
    for layer in range(DEPTH):
        if layer < N_A_LAYERS:
            i = layer
            x, r = retention_layer(x, c, qpos, r0[i], a_norm_g[i], a_ada_w[i], a_ada_b[i],
                                   a_w_in[i], a_gn_g[i], a_w_out[i])
            r_out.append(r)
        else:
            if layer == N_A_LAYERS:
                k_new, v_new, logf_new = shared_kv(x, c, kv_norm_g, kv_ada_w, kv_ada_b, kv_w, kv_fb, k_norm_g)
                if k_past is None:
                    k_all, v_all, logf_all = k_new, v_new, logf_new
                else:
                    k_all = jnp.concatenate([k_past.astype(k_new.dtype), k_new], axis=1)
                    v_all = jnp.concatenate([v_past.astype(v_new.dtype), v_new], axis=1)
                    logf_all = jnp.concatenate([logf_past.astype(F32), logf_new], axis=1)
                F = jnp.cumsum(logf_all, axis=1)
                F_q = F[:, F.shape[1] - L:]
                kpos = jnp.arange(F.shape[1])
            j = layer - N_A_LAYERS
            x = fox_layer(x, c, k_all, v_all, F, F_q, qpos, kpos, b_norm_g[j], b_ada_w[j], b_ada_b[j],
                          b_w_in[j], b_q_norm_g[j], b_w_out[j])
    return x, jnp.stack(r_out, axis=0), k_new, v_new, logf_new


def setup_inputs(seed: int = 0) -> dict:
    key = jax.random.key(seed)
    ks = jax.random.split(key, 32)
    D = D_MODEL
    nrm = jax.random.normal
    s_d = D ** -0.5
    return {
        'x_prompt': nrm(ks[0], (BATCH, SEQ, D), F32),
        'x_sample': nrm(ks[1], (DEC_BATCH, DEC_SEQ, D), F32),
        'state_ret': 0.05 * nrm(ks[2], (N_A_LAYERS, DEC_BATCH, RET_HEADS, RET_DK, RET_DV), F32),
        'cache_k': nrm(ks[3], (DEC_BATCH, PAST_LEN, FOX_KV_HEADS, FOX_DH), F32),
        'cache_v': nrm(ks[4], (DEC_BATCH, PAST_LEN, FOX_KV_HEADS, FOX_DH), F32),
        'cache_logf': jax.nn.log_sigmoid(2.0 + nrm(ks[5], (DEC_BATCH, PAST_LEN, FOX_HEADS), F32)),
        'c_prompt': nrm(ks[6], (BATCH, D), F32),
        'c_sample': nrm(ks[7], (DEC_BATCH, D), F32),
        'a_norm_g': 1.0 + 0.02 * nrm(ks[8], (N_A_LAYERS, D), F32),
        'a_ada_w': 0.5 * s_d * nrm(ks[9], (N_A_LAYERS, D, 3 * D), F32),
        'a_ada_b': 0.02 * nrm(ks[10], (N_A_LAYERS, 3 * D), F32),
        'a_w_in': s_d * nrm(ks[11], (N_A_LAYERS, D, RET_IN), F32),
        'a_gn_g': 1.0 + 0.02 * nrm(ks[12], (N_A_LAYERS, RET_VW), F32),
        'a_w_out': (RET_VW ** -0.5) * nrm(ks[13], (N_A_LAYERS, RET_VW, D), F32),
        'kv_norm_g': 1.0 + 0.02 * nrm(ks[14], (D,), F32),
        'kv_ada_w': 0.5 * s_d * nrm(ks[15], (D, 2 * D), F32),
        'kv_ada_b': 0.02 * nrm(ks[16], (2 * D,), F32),
        'kv_w': s_d * nrm(ks[17], (D, KV_OUT), F32),
        'kv_fb': 2.0 + 0.1 * nrm(ks[18], (FOX_HEADS,), F32),
        'k_norm_g': 1.0 + 0.02 * nrm(ks[19], (FOX_DH,), F32),
        'b_norm_g': 1.0 + 0.02 * nrm(ks[20], (N_B_LAYERS, D), F32),
        'b_ada_w': 0.5 * s_d * nrm(ks[21], (N_B_LAYERS, D, 3 * D), F32),
        'b_ada_b': 0.02 * nrm(ks[22], (N_B_LAYERS, 3 * D), F32),
        'b_w_in': s_d * nrm(ks[23], (N_B_LAYERS, D, FOX_IN), F32),
        'b_q_norm_g': 1.0 + 0.02 * nrm(ks[24], (N_B_LAYERS, FOX_DH), F32),
        'b_w_out': (FOX_W ** -0.5) * nrm(ks[25], (N_B_LAYERS, FOX_W, D), F32),
    }


def reference(x_prompt, x_sample, state_ret, cache_k, cache_v, cache_logf, c_prompt, c_sample,
              a_norm_g, a_ada_w, a_ada_b, a_w_in, a_gn_g, a_w_out,
              kv_norm_g, kv_ada_w, kv_ada_b, kv_w, kv_fb, k_norm_g,
              b_norm_g, b_ada_w, b_ada_b, b_w_in, b_q_norm_g, b_w_out):
    weights = (a_norm_g, a_ada_w, a_ada_b, a_w_in, a_gn_g, a_w_out,
               kv_norm_g, kv_ada_w, kv_ada_b, kv_w, kv_fb, k_norm_g,
               b_norm_g, b_ada_w, b_ada_b, b_w_in, b_q_norm_g, b_w_out)
    r0_p = jnp.zeros((N_A_LAYERS, x_prompt.shape[0], RET_HEADS, RET_DK, RET_DV), F32)
    y_prompt, state_ret_prompt, k_prompt, v_prompt, logf_prompt = trunk(
        x_prompt, c_prompt, 0, r0_p, None, None, None, *weights)
    past = cache_k.shape[1]
    y_sample, state_ret_sample, k_sample, v_sample, logf_sample = trunk(
        x_sample, c_sample, past, state_ret, cache_k, cache_v, cache_logf, *weights)
    return (y_prompt, y_sample, state_ret_prompt, k_prompt, v_prompt, logf_prompt,
            state_ret_sample, k_sample, v_sample, logf_sample)
```

```python
import functools
import math

import numpy as np
import jax
import jax.numpy as jnp
from jax import lax
from jax.experimental import pallas as pl
from jax.experimental.pallas import tpu as pltpu

F32 = jnp.float32
BF16 = jnp.bfloat16
EPS = 1e-6
ROPE_BASE = 10000.0
LANES = 128
VMEM_LIMIT = 56 * 1024 * 1024

_NT = (((1,), (1,)), ((), ()))
_TN = (((0,), (0,)), ((), ()))


def _const_spec(shape):
    return pl.BlockSpec(shape, lambda *_: (0,) * len(shape), pipeline_mode=pl.Buffered(1))


def _silu(x):
    return x / (1.0 + jnp.exp(-x))


def _log_sigmoid(x):
    return jnp.minimum(x, 0.0) - jnp.log(1.0 + jnp.exp(-jnp.abs(x)))


def _modulated_norm(x, g, shift, scale):
    xh = x * lax.rsqrt(jnp.mean(x * x, axis=-1, keepdims=True) + EPS)
    return (xh * g) * (1.0 + scale) + shift


def _tri_cumsum(lf):
    t = lf.shape[0]
    row = lax.broadcasted_iota(jnp.int32, (t, t), 0)
    col = lax.broadcasted_iota(jnp.int32, (t, t), 1)
    tri = jnp.where(col <= row, 1.0, 0.0).astype(BF16)
    hi = lf.astype(BF16)
    r1 = lf - hi.astype(F32)
    mid = r1.astype(BF16)
    lo = (r1 - mid.astype(F32)).astype(BF16)
    out = jnp.dot(tri, hi, preferred_element_type=F32)
    out = out + jnp.dot(tri, mid, preferred_element_type=F32)
    return out + jnp.dot(tri, lo, preferred_element_type=F32)


def _ada_kernel(c_ref, w_ref, b_ref, o_ref):
    s = _silu(c_ref[...]).astype(BF16)
    o_ref[...] = jnp.dot(s, w_ref[...].astype(BF16), preferred_element_type=F32) + b_ref[...]


def _ada(c, w, b, *, tn=1024):
    m, d = c.shape
    n = w.shape[1]
    return pl.pallas_call(
        _ada_kernel,
        out_shape=jax.ShapeDtypeStruct((m, n), F32),
        grid=(n // tn,),
        in_specs=[pl.BlockSpec((m, d), lambda j: (0, 0)),
                  pl.BlockSpec((d, tn), lambda j: (0, j)),
                  pl.BlockSpec((1, tn), lambda j: (0, j))],
        out_specs=pl.BlockSpec((m, tn), lambda j: (0, j)),
        compiler_params=pltpu.CompilerParams(dimension_semantics=("arbitrary",)),
        name="ada",
    )(c, w, b.reshape(1, n))


def _ret_kernel(*refs, T, C, H, DK, DV, D, has_state):
    if has_state:
        (x_ref, mod_ref, cos_ref, sin_ref, ng_ref, win_ref, gng_ref, wout_ref, r0_ref,
         y_ref, r_ref) = refs
    else:
        (x_ref, mod_ref, cos_ref, sin_ref, ng_ref, win_ref, gng_ref, wout_ref,
         y_ref, r_ref) = refs
        r0_ref = None

    @pl.when(pl.program_id(1) == 0)
    def _():
        if has_state:
            r_ref[...] = r0_ref[...]
        else:
            r_ref[...] = jnp.zeros_like(r_ref)

    x = x_ref[0]
    mod = mod_ref[0]
    shift, scale, gate = mod[:, :D], mod[:, D:2 * D], mod[:, 2 * D:]
    h = _modulated_norm(x, ng_ref[...], shift, scale).astype(BF16)
    cos = cos_ref[...]
    sin = sin_ref[...]
    half = DK // 2
    qk_w = H * DK

    row = lax.broadcasted_iota(jnp.int32, (C, C), 0)
    col = lax.broadcasted_iota(jnp.int32, (C, C), 1)
    diff = row - col
    diff_f = jnp.maximum(diff, 0).astype(F32)
    idx = lax.broadcasted_iota(jnp.int32, (C, 1), 0).astype(F32)

    def rot(t):
        t1, t2 = t[:, :half], t[:, half:]
        return jnp.concatenate([t1 * cos - t2 * sin, t2 * cos + t1 * sin], axis=-1)

    acc = jnp.zeros((T, D), F32)
    for hd in range(H):
        lg = math.log1p(-(2.0 ** (-5.0 - hd)))
        q = rot(jnp.dot(h, win_ref[:, hd * DK:(hd + 1) * DK], preferred_element_type=F32))
        k = rot(jnp.dot(h, win_ref[:, qk_w + hd * DK:qk_w + (hd + 1) * DK],
                        preferred_element_type=F32)) * (DK ** -0.5)
        v = jnp.dot(h, win_ref[:, 2 * qk_w + hd * DV:2 * qk_w + (hd + 1) * DV],
                    preferred_element_type=F32).astype(BF16)
        g = jnp.dot(h, win_ref[:, 2 * qk_w + H * DV + hd * DV:2 * qk_w + H * DV + (hd + 1) * DV],
                    preferred_element_type=F32)
        qb = q.astype(BF16)
        kb = k.astype(BF16)
        intra = jnp.where(diff >= 0, jnp.exp(diff_f * lg), 0.0)
        q_dec = jnp.exp((idx + 1.0) * lg)
        k_dec = jnp.exp((C - 1.0 - idx) * lg)
        c_dec = math.exp(C * lg)
        outs = []
        for c in range(T // C):
            sl = slice(c * C, (c + 1) * C)
            r = r_ref[0, hd]
            s = lax.dot_general(qb[sl], kb[sl], _NT, preferred_element_type=F32) * intra
            o = jnp.dot(s.astype(BF16), v[sl], preferred_element_type=F32)
            o = o + jnp.dot(qb[sl], r.astype(BF16), preferred_element_type=F32) * q_dec
            kd = (k[sl] * k_dec).astype(BF16)
            r_ref[0, hd] = r * c_dec + lax.dot_general(kd, v[sl], _TN, preferred_element_type=F32)
            outs.append(o)
        o = outs[0] if len(outs) == 1 else jnp.concatenate(outs, axis=0)
        mu = jnp.mean(o, axis=-1, keepdims=True)
        oc = o - mu
        var = jnp.mean(oc * oc, axis=-1, keepdims=True)
        on = oc * lax.rsqrt(var + EPS) * gng_ref[:, hd * DV:(hd + 1) * DV]
        og = (on * _silu(g)).astype(BF16)
        acc = acc + jnp.dot(og, wout_ref[hd * DV:(hd + 1) * DV, :], preferred_element_type=F32)
    y_ref[0] = x + gate * acc


def _ret_layer(x, mod, cos, sin, norm_g, w_in, gn_g, w_out, r0, *, T, C, H, DK, DV):
    B, L, D = x.shape
    has_state = r0 is not None
    kern = functools.partial(_ret_kernel, T=T, C=C, H=H, DK=DK, DV=DV, D=D, has_state=has_state)
    in_specs = [
        pl.BlockSpec((1, T, D), lambda b, t: (b, t, 0)),
        pl.BlockSpec((1, 1, 3 * D), lambda b, t: (b, 0, 0)),
        pl.BlockSpec((T, DK // 2), lambda b, t: (t, 0)),
        pl.BlockSpec((T, DK // 2), lambda b, t: (t, 0)),
        _const_spec((1, D)),
        _const_spec(w_in.shape),
        _const_spec((1, H * DV)),
        _const_spec(w_out.shape),
    ]
    args = [x, mod.reshape(B, 1, 3 * D), cos, sin, norm_g.reshape(1, D), w_in,
            gn_g.reshape(1, H * DV), w_out]
    if has_state:
        in_specs.append(pl.BlockSpec((1, H, DK, DV), lambda b, t: (b, 0, 0, 0)))
        args.append(r0)
    return pl.pallas_call(
        kern,
        out_shape=(jax.ShapeDtypeStruct((B, L, D), F32),
                   jax.ShapeDtypeStruct((B, H, DK, DV), F32)),
        grid=(B, L // T),
        in_specs=in_specs,
        out_specs=(pl.BlockSpec((1, T, D), lambda b, t: (b, t, 0)),
                   pl.BlockSpec((1, H, DK, DV), lambda b, t: (b, 0, 0, 0))),
        compiler_params=pltpu.CompilerParams(dimension_semantics=("arbitrary", "arbitrary"),
                                             vmem_limit_bytes=VMEM_LIMIT),
        name="ret_layer",
    )(*args)


def _cumsum_kernel(lf_ref, f_ref, carry_ref):
    @pl.when(pl.program_id(1) == 0)
    def _():
        carry_ref[...] = jnp.zeros_like(carry_ref)

    f = _tri_cumsum(lf_ref[0]) + carry_ref[...]
    f_ref[0] = f
    carry_ref[...] = f[-1:, :]


def _cumsum(lf, *, T):
    B, S, W = lf.shape
    return pl.pallas_call(
        _cumsum_kernel,
        out_shape=jax.ShapeDtypeStruct((B, S, W), F32),
        grid=(B, S // T),
        in_specs=[pl.BlockSpec((1, T, W), lambda b, t: (b, t, 0))],
        out_specs=pl.BlockSpec((1, T, W), lambda b, t: (b, t, 0)),
        scratch_shapes=[pltpu.VMEM((1, W), F32)],
        compiler_params=pltpu.CompilerParams(dimension_semantics=("arbitrary", "arbitrary")),
        name="cumsum_logf",
    )(lf)


def _prep_kernel(x_ref, kvmod_ref, bmod_ref, f0_ref, kvng_ref, bng_ref, kvw_ref, fw_ref, fb_ref,
                 kng_ref, bwin_ref, qng_ref,
                 k_ref, v_ref, lf_ref, fc_ref, kb_ref, vb_ref, qb_ref, gb_ref,
                 carry_ref, *, D, HKV, HQ, DH):
    @pl.when(pl.program_id(1) == 0)
    def _():
        carry_ref[...] = f0_ref[0]

    x = x_ref[0]
    kvmod = kvmod_ref[0]
    hk = _modulated_norm(x, kvng_ref[...], kvmod[:, :D], kvmod[:, D:]).astype(BF16)
    kv_w = HKV * DH
    kng = kng_ref[...]
    for hd in range(HKV):
        kh = jnp.dot(hk, kvw_ref[:, hd * DH:(hd + 1) * DH], preferred_element_type=F32)
        kh = kh * lax.rsqrt(jnp.mean(kh * kh, axis=-1, keepdims=True) + EPS) * kng
        k_ref[0, :, hd * DH:(hd + 1) * DH] = kh
        kb_ref[0, :, hd * DH:(hd + 1) * DH] = kh.astype(BF16)
    v = jnp.dot(hk, kvw_ref[:, kv_w:], preferred_element_type=F32)
    v_ref[0] = v
    vb_ref[0] = v.astype(BF16)
    f = jnp.dot(hk, fw_ref[...], preferred_element_type=F32)
    lf = _log_sigmoid(f + fb_ref[...])
    lf_ref[0] = lf
    fc = _tri_cumsum(lf) + carry_ref[...]
    fc_ref[0] = fc
    carry_ref[...] = fc[-1:, :]

    bmod = bmod_ref[0]
    hq = _modulated_norm(x, bng_ref[...], bmod[:, :D], bmod[:, D:2 * D]).astype(BF16)
    qng = qng_ref[...] * (DH ** -0.5)
    fox_w = HQ * DH
    for hd in range(HQ):
        qh = jnp.dot(hq, bwin_ref[:, hd * DH:(hd + 1) * DH], preferred_element_type=F32)
        qh = qh * lax.rsqrt(jnp.mean(qh * qh, axis=-1, keepdims=True) + EPS) * qng
        qb_ref[0, :, hd * DH:(hd + 1) * DH] = qh.astype(BF16)
    g = jnp.dot(hq, bwin_ref[:, fox_w:], preferred_element_type=F32)
    gb_ref[0] = _silu(g).astype(BF16)


def _prep(x1, kvmod, bmod, f0, kv_norm_g, b_norm_g, kvw, fw, fb, k_norm_g, b_w_in, q_norm_g,
          *, T, HKV, HQ, DH):
    B, L, D = x1.shape
    kv_w = HKV * DH
    fox_w = HQ * DH
    kern = functools.partial(_prep_kernel, D=D, HKV=HKV, HQ=HQ, DH=DH)
    tok = lambda w: pl.BlockSpec((1, T, w), lambda b, t: (b, t, 0))
    per_b = lambda w: pl.BlockSpec((1, 1, w), lambda b, t: (b, 0, 0))
    return pl.pallas_call(
        kern,
        out_shape=(jax.ShapeDtypeStruct((B, L, kv_w), F32),
                   jax.ShapeDtypeStruct((B, L, kv_w), F32),
                   jax.ShapeDtypeStruct((B, L, LANES), F32),
                   jax.ShapeDtypeStruct((B, L, LANES), F32),
                   jax.ShapeDtypeStruct((B, L, kv_w), BF16),
                   jax.ShapeDtypeStruct((B, L, kv_w), BF16),
                   jax.ShapeDtypeStruct((B, L, fox_w), BF16),
                   jax.ShapeDtypeStruct((B, L, fox_w), BF16)),
        grid=(B, L // T),
        in_specs=[tok(D), per_b(2 * D), per_b(3 * D), per_b(LANES),
                  _const_spec((1, D)), _const_spec((1, D)),
                  _const_spec(kvw.shape), _const_spec(fw.shape), _const_spec((1, LANES)),
                  _const_spec((1, DH)), _const_spec(b_w_in.shape), _const_spec((1, DH))],
        out_specs=(tok(kv_w), tok(kv_w), tok(LANES), tok(LANES), tok(kv_w), tok(kv_w),
                   tok(fox_w), tok(fox_w)),
        scratch_shapes=[pltpu.VMEM((1, LANES), F32)],
        compiler_params=pltpu.CompilerParams(dimension_semantics=("arbitrary", "arbitrary"),
                                             vmem_limit_bytes=VMEM_LIMIT),
        name="prep",
    )(x1, kvmod.reshape(B, 1, 2 * D), bmod.reshape(B, 1, 3 * D), f0, kv_norm_g.reshape(1, D),
      b_norm_g.reshape(1, D), kvw, fw, fb, k_norm_g.reshape(1, DH), b_w_in, q_norm_g.reshape(1, DH))


def _attn_kernel(qi_ref, kj_ref, first_ref, last_ref, mask_ref,
                 q_ref, k_ref, v_ref, fq_ref, fk_ref, o_ref,
                 m_ref, l_ref, acc_ref, *, tq, tk, HQ, G, DH, q_off):
    s_id = pl.program_id(1)

    @pl.when(first_ref[s_id] == 1)
    def _():
        m_ref[...] = jnp.full_like(m_ref, -jnp.inf)
        l_ref[...] = jnp.zeros_like(l_ref)
        acc_ref[...] = jnp.zeros_like(acc_ref)

    def step(masked):
        if masked:
            qpos = q_off + qi_ref[s_id] * tq + lax.broadcasted_iota(jnp.int32, (tq, tk), 0)
            kpos = kj_ref[s_id] * tk + lax.broadcasted_iota(jnp.int32, (tq, tk), 1)
            keep = kpos <= qpos
        for hd in range(HQ):
            kvh = hd // G
            q = q_ref[0, :, hd * DH:(hd + 1) * DH]
            k = k_ref[0, :, kvh * DH:(kvh + 1) * DH]
            s = lax.dot_general(q, k, _NT, preferred_element_type=F32)
            s = s + fq_ref[0, :, hd:hd + 1] - fk_ref[0, hd:hd + 1, :]
            if masked:
                s = jnp.where(keep, s, -jnp.inf)
            m_prev = m_ref[hd]
            m_new = jnp.maximum(m_prev, jnp.max(s, axis=-1, keepdims=True))
            alpha = jnp.exp(m_prev - m_new)
            p = jnp.exp(s - m_new)
            l_ref[hd] = alpha * l_ref[hd] + jnp.sum(p, axis=-1, keepdims=True)
            m_ref[hd] = m_new
            pv = jnp.dot(p.astype(BF16), v_ref[0, :, kvh * DH:(kvh + 1) * DH],
                         preferred_element_type=F32)
            acc_ref[:, hd * DH:(hd + 1) * DH] = alpha * acc_ref[:, hd * DH:(hd + 1) * DH] + pv

    @pl.when(mask_ref[s_id] == 1)
    def _():
        step(True)

    @pl.when(mask_ref[s_id] == 0)
    def _():
        step(False)

    @pl.when(last_ref[s_id] == 1)
    def _():
        for hd in range(HQ):
            o_ref[0, :, hd * DH:(hd + 1) * DH] = (
                acc_ref[:, hd * DH:(hd + 1) * DH] / l_ref[hd]).astype(o_ref.dtype)


def _attn_tables(L, S, tq, tk, q_off):
    qi, kj, first, last, mask = [], [], [], [], []
    for i in range(L // tq):
        q_lo, q_hi = q_off + i * tq, q_off + (i + 1) * tq - 1
        js = [j for j in range(S // tk) if j * tk <= q_hi]
        for n, j in enumerate(js):
            qi.append(i)
            kj.append(j)
            first.append(int(n == 0))
            last.append(int(n == len(js) - 1))
            mask.append(int((j + 1) * tk - 1 > q_lo))
    return [jnp.asarray(np.asarray(a, np.int32)) for a in (qi, kj, first, last, mask)]


def _attn(qb, kb, vb, fq, fk_t, *, tq, tk, HQ, HKV, DH, q_off):
    B, L, fox_w = qb.shape
    S = kb.shape[1]
    kv_w = HKV * DH
    tables = _attn_tables(L, S, tq, tk, q_off)
    n_steps = int(tables[0].shape[0])
    kern = functools.partial(_attn_kernel, tq=tq, tk=tk, HQ=HQ, G=HQ // HKV, DH=DH, q_off=q_off)
    grid_spec = pltpu.PrefetchScalarGridSpec(
        num_scalar_prefetch=5,
        grid=(B, n_steps),
        in_specs=[
            pl.BlockSpec((1, tq, fox_w), lambda b, s, qi, kj, *_: (b, qi[s], 0)),
            pl.BlockSpec((1, tk, kv_w), lambda b, s, qi, kj, *_: (b, kj[s], 0)),
            pl.BlockSpec((1, tk, kv_w), lambda b, s, qi, kj, *_: (b, kj[s], 0)),
            pl.BlockSpec((1, tq, LANES), lambda b, s, qi, kj, *_: (b, qi[s], 0)),
            pl.BlockSpec((1, HQ, tk), lambda b, s, qi, kj, *_: (b, 0, kj[s])),
        ],
        out_specs=pl.BlockSpec((1, tq, fox_w), lambda b, s, qi, kj, *_: (b, qi[s], 0)),
        scratch_shapes=[pltpu.VMEM((HQ, tq, 1), F32), pltpu.VMEM((HQ, tq, 1), F32),
                        pltpu.VMEM((tq, fox_w), F32)],
    )
    return pl.pallas_call(
        kern,
        out_shape=jax.ShapeDtypeStruct((B, L, fox_w), BF16),
        grid_spec=grid_spec,
        compiler_params=pltpu.CompilerParams(dimension_semantics=("arbitrary", "arbitrary"),
                                             vmem_limit_bytes=VMEM_LIMIT),
        name="fox_attn",
    )(*tables, qb, kb, vb, fq, fk_t)


def _fout_kernel(x_ref, o_ref, g_ref, mod_ref, w_ref, y_ref, *, D):
    og = (o_ref[0].astype(F32) * g_ref[0].astype(F32)).astype(BF16)
    gate = mod_ref[0][:, 2 * D:]
    y_ref[0] = x_ref[0] + gate * jnp.dot(og, w_ref[...], preferred_element_type=F32)


def _fout(x1, o, gb, bmod, w_out, *, T):
    B, L, D = x1.shape
    W = o.shape[-1]
    return pl.pallas_call(
        functools.partial(_fout_kernel, D=D),
        out_shape=jax.ShapeDtypeStruct((B, L, D), F32),
        grid=(B, L // T),
        in_specs=[pl.BlockSpec((1, T, D), lambda b, t: (b, t, 0)),
                  pl.BlockSpec((1, T, W), lambda b, t: (b, t, 0)),
                  pl.BlockSpec((1, T, W), lambda b, t: (b, t, 0)),
                  pl.BlockSpec((1, 1, 3 * D), lambda b, t: (b, 0, 0)),
                  _const_spec(w_out.shape)],
        out_specs=pl.BlockSpec((1, T, D), lambda b, t: (b, t, 0)),
        compiler_params=pltpu.CompilerParams(dimension_semantics=("arbitrary", "arbitrary")),
        name="fox_out",
    )(x1, o, gb, bmod.reshape(B, 1, 3 * D), w_out)


def _rope_tables(pos0, L, half):
    inv = jnp.power(ROPE_BASE, -jnp.arange(half, dtype=F32) / half)
    ang = (pos0 + jnp.arange(L)).astype(F32)[:, None] * inv[None, :]
    return jnp.cos(ang), jnp.sin(ang)


def _pad_lanes(a):
    return jnp.pad(a, [(0, 0)] * (a.ndim - 1) + [(0, LANES - a.shape[-1])])


def _trunk(x, a_mod, kv_mod, b_mod, pos0, r0, past, wts, dims, tiles):
    H, DK, DV, HKV, HQ, DH = dims
    B, L, D = x.shape
    cos, sin = _rope_tables(pos0, L, DK // 2)
    x1, r = _ret_layer(x, a_mod, cos, sin, wts["a_norm_g"], wts["a_w_in"], wts["a_gn_g"],
                       wts["a_w_out"], r0, T=tiles["ret_t"], C=tiles["ret_c"],
                       H=H, DK=DK, DV=DV)
    if past is None:
        f0 = jnp.zeros((B, 1, LANES), F32)
    else:
        k_past, v_past, lf_past = past
        f_past = _cumsum(_pad_lanes(lf_past.astype(F32)), T=tiles["past_t"])
        f0 = f_past[:, -1:, :]
    k, v, lf, fc, kb, vb, qb, gb = _prep(
        x1, kv_mod, b_mod, f0, wts["kv_norm_g"], wts["b_norm_g"], wts["kvw"], wts["fw"],
        wts["fb"], wts["k_norm_g"], wts["b_w_in"], wts["b_q_norm_g"],
        T=tiles["prep_t"], HKV=HKV, HQ=HQ, DH=DH)
    if past is None:
        kb_all, vb_all, f_all, q_off = kb, vb, fc, 0
    else:
        S = k_past.shape[1] + L
        pad = (-S) % tiles["attn_tk"]
        kv_w = HKV * DH
        cat = lambda p, n: jnp.pad(jnp.concatenate([p, n], axis=1), ((0, 0), (0, pad), (0, 0)))
        kb_all = cat(k_past.reshape(B, -1, kv_w).astype(BF16), kb)
        vb_all = cat(v_past.reshape(B, -1, kv_w).astype(BF16), vb)
        f_all = cat(f_past, fc)
        q_off = k_past.shape[1]
    fk_t = jnp.swapaxes(f_all[:, :, :HQ], 1, 2)
    o = _attn(qb, kb_all, vb_all, fc, fk_t, tq=tiles["attn_tq"], tk=tiles["attn_tk"],
              HQ=HQ, HKV=HKV, DH=DH, q_off=q_off)
    y = _fout(x1, o, gb, b_mod, wts["b_w_out"], T=tiles["prep_t"])
    return (y, r[None], k.reshape(B, L, HKV, DH), v.reshape(B, L, HKV, DH), lf[:, :, :HQ])


def kernel(x_prompt, x_sample, state_ret, cache_k, cache_v, cache_logf, c_prompt, c_sample, a_norm_g, a_ada_w, a_ada_b, a_w_in, a_gn_g, a_w_out, kv_norm_g, kv_ada_w, kv_ada_b, kv_w, kv_fb, k_norm_g, b_norm_g, b_ada_w, b_ada_b, b_w_in, b_q_norm_g, b_w_out):
    assert a_w_in.shape[0] == 1 and b_w_in.shape[0] == 1, "one retention and one attention layer"
    Bp, Lp, D = x_prompt.shape
    Bs, Ls, _ = x_sample.shape
    _, _, H, DK, DV = state_ret.shape
    _, past_len, HKV, DH = cache_k.shape
    HQ = cache_logf.shape[-1]
    dims = (H, DK, DV, HKV, HQ, DH)
    kv_wd = HKV * DH

    c_all = jnp.concatenate([c_prompt, c_sample], axis=0)
    a_mod = _ada(c_all, a_ada_w[0], a_ada_b[0])
    kv_mod = _ada(c_all, kv_ada_w, kv_ada_b)
    b_mod = _ada(c_all, b_ada_w[0], b_ada_b[0])

    wts = dict(
        a_norm_g=a_norm_g[0], a_w_in=a_w_in[0].astype(BF16), a_gn_g=a_gn_g[0],
        a_w_out=a_w_out[0].astype(BF16), kv_norm_g=kv_norm_g,
        kvw=kv_w[:, :2 * kv_wd].astype(BF16), fw=_pad_lanes(kv_w[:, 2 * kv_wd:]).astype(BF16),
        fb=_pad_lanes(kv_fb.reshape(1, HQ)), k_norm_g=k_norm_g, b_norm_g=b_norm_g[0],
        b_w_in=b_w_in[0].astype(BF16), b_q_norm_g=b_q_norm_g[0], b_w_out=b_w_out[0].astype(BF16))

    p_tiles = dict(ret_t=min(512, Lp), ret_c=min(256, Lp), prep_t=min(512, Lp),
                   attn_tq=min(512, Lp), attn_tk=min(512, Lp))
    yp, rp, kp, vp, lfp = _trunk(x_prompt, a_mod[:Bp], kv_mod[:Bp], b_mod[:Bp], 0, None, None,
                                 wts, dims, p_tiles)
    s_total = past_len + Ls
    s_tiles = dict(ret_t=Ls, ret_c=Ls, prep_t=Ls, attn_tq=Ls,
                   attn_tk=-(-s_total // LANES) * LANES, past_t=min(512, past_len))
    ys, rs, ks, vs, lfs = _trunk(x_sample, a_mod[Bp:], kv_mod[Bp:], b_mod[Bp:], past_len,
                                 state_ret[0], (cache_k, cache_v, cache_logf), wts, dims, s_tiles)
    return (yp, ys, rp, kp, vp, lfp, rs, ks, vs, lfs)
```

```python
import functools
import math

import numpy as np
import jax
import jax.numpy as jnp
from jax import lax
from jax.experimental import pallas as pl
from jax.experimental.pallas import tpu as pltpu

F32 = jnp.float32
BF16 = jnp.bfloat16
EPS = 1e-6
ROPE_BASE = 10000.0
LOG2E = 1.4426950408889634
LANES = 128
VMEM_LIMIT = 56 * 1024 * 1024
M_INIT = -1e30

_NT = (((1,), (1,)), ((), ()))
_TN = (((0,), (0,)), ((), ()))


def _const_spec(shape):
    return pl.BlockSpec(shape, lambda *_: (0,) * len(shape), pipeline_mode=pl.Buffered(1))


def _silu(x):
    return x / (1.0 + jnp.exp(-x))


def _log_sigmoid(x):
    return jnp.minimum(x, 0.0) - jnp.log(1.0 + jnp.exp(-jnp.abs(x)))


def _modulated_norm(x, g, shift, scale):
    xh = x * lax.rsqrt(jnp.mean(x * x, axis=-1, keepdims=True) + EPS)
    return (xh * g) * (1.0 + scale) + shift


def _split3(x):
    hi = x.astype(BF16)
    r1 = x - hi.astype(F32)
    mid = r1.astype(BF16)
    lo = (r1 - mid.astype(F32)).astype(BF16)
    return hi, mid, lo


def _tri_cumsum(lf):
    t = lf.shape[0]
    row = lax.broadcasted_iota(jnp.int32, (t, t), 0)
    col = lax.broadcasted_iota(jnp.int32, (t, t), 1)
    tri = jnp.where(col <= row, 1.0, 0.0).astype(BF16)
    hi, mid, lo = _split3(lf)
    out = jnp.dot(tri, hi, preferred_element_type=F32)
    out = out + jnp.dot(tri, mid, preferred_element_type=F32)
    return out + jnp.dot(tri, lo, preferred_element_type=F32)


def _lane_onehot(rows, lo, hi):
    lane = lax.broadcasted_iota(jnp.int32, (rows, LANES), 1)
    return jnp.where((lane >= lo) & (lane < hi), 1.0, 0.0).astype(BF16)


def _key_bias_cols(fc, place_ref):
    hi, mid, lo = _split3(fc * (-LOG2E))
    cat = jnp.concatenate([hi, mid, lo], axis=-1)
    return jnp.dot(cat, place_ref[...], preferred_element_type=F32).astype(BF16)


def _place_matrix(HKV, G):
    p = np.zeros((3 * LANES, HKV * LANES), np.float32)
    for kvh in range(HKV):
        for g in range(G):
            for j in range(3):
                p[j * LANES + kvh * G + g, kvh * LANES + 3 * g + j] = 1.0
    return jnp.asarray(p, BF16)


def _ada_kernel(c_ref, w_ref, b_ref, o_ref):
    s = _silu(c_ref[...]).astype(BF16)
    o_ref[...] = jnp.dot(s, w_ref[...].astype(BF16), preferred_element_type=F32) + b_ref[...]


def _ada(c, w, b, *, tn=1024):
    m, d = c.shape
    n = w.shape[1]
    return pl.pallas_call(
        _ada_kernel,
        out_shape=jax.ShapeDtypeStruct((m, n), F32),
        grid=(n // tn,),
        in_specs=[pl.BlockSpec((m, d), lambda j: (0, 0)),
                  pl.BlockSpec((d, tn), lambda j: (0, j)),
                  pl.BlockSpec((1, tn), lambda j: (0, j))],
        out_specs=pl.BlockSpec((m, tn), lambda j: (0, j)),
        compiler_params=pltpu.CompilerParams(dimension_semantics=("arbitrary",)),
        name="ada",
    )(c, w, b.reshape(1, n))


def _ret_kernel(*refs, T, C, H, DK, DV, D, has_state):
    if has_state:
        (x_ref, mod_ref, cos_ref, sin_ref, ng_ref, win_ref, gng_ref, wout_ref, r0_ref,
         y_ref, r_ref) = refs
    else:
        (x_ref, mod_ref, cos_ref, sin_ref, ng_ref, win_ref, gng_ref, wout_ref,
         y_ref, r_ref) = refs
        r0_ref = None

    @pl.when(pl.program_id(1) == 0)
    def _():
        if has_state:
            r_ref[...] = r0_ref[...]
        else:
            r_ref[...] = jnp.zeros_like(r_ref)

    x = x_ref[0]
    mod = mod_ref[0]
    shift, scale, gate = mod[:, :D], mod[:, D:2 * D], mod[:, 2 * D:]
    h = _modulated_norm(x, ng_ref[...], shift, scale).astype(BF16)
    cos = cos_ref[...]
    sin = sin_ref[...]
    half = DK // 2
    qk_w = H * DK

    row = lax.broadcasted_iota(jnp.int32, (C, C), 0)
    col = lax.broadcasted_iota(jnp.int32, (C, C), 1)
    diff = row - col
    diff_f = jnp.maximum(diff, 0).astype(F32)
    idx = lax.broadcasted_iota(jnp.int32, (C, 1), 0).astype(F32)

    def rot(t):
        t1, t2 = t[:, :half], t[:, half:]
        return jnp.concatenate([t1 * cos - t2 * sin, t2 * cos + t1 * sin], axis=-1)

    acc = jnp.zeros((T, D), F32)
    for hd in range(H):
        lg = math.log1p(-(2.0 ** (-5.0 - hd)))
        q = rot(jnp.dot(h, win_ref[:, hd * DK:(hd + 1) * DK], preferred_element_type=F32))
        k = rot(jnp.dot(h, win_ref[:, qk_w + hd * DK:qk_w + (hd + 1) * DK],
                        preferred_element_type=F32)) * (DK ** -0.5)
        v = jnp.dot(h, win_ref[:, 2 * qk_w + hd * DV:2 * qk_w + (hd + 1) * DV],
                    preferred_element_type=F32).astype(BF16)
        g = jnp.dot(h, win_ref[:, 2 * qk_w + H * DV + hd * DV:2 * qk_w + H * DV + (hd + 1) * DV],
                    preferred_element_type=F32)
        qb = q.astype(BF16)
        kb = k.astype(BF16)
        intra = jnp.where(diff >= 0, jnp.exp(diff_f * lg), 0.0)
        q_dec = jnp.exp((idx + 1.0) * lg)
        k_dec = jnp.exp((C - 1.0 - idx) * lg)
        c_dec = math.exp(C * lg)
        outs = []
        for c in range(T // C):
            sl = slice(c * C, (c + 1) * C)
            r = r_ref[0, hd]
            s = lax.dot_general(qb[sl], kb[sl], _NT, preferred_element_type=F32) * intra
            o = jnp.dot(s.astype(BF16), v[sl], preferred_element_type=F32)
            o = o + jnp.dot(qb[sl], r.astype(BF16), preferred_element_type=F32) * q_dec
            kd = (k[sl] * k_dec).astype(BF16)
            r_ref[0, hd] = r * c_dec + lax.dot_general(kd, v[sl], _TN, preferred_element_type=F32)
            outs.append(o)
        o = outs[0] if len(outs) == 1 else jnp.concatenate(outs, axis=0)
        mu = jnp.mean(o, axis=-1, keepdims=True)
        oc = o - mu
        var = jnp.mean(oc * oc, axis=-1, keepdims=True)
        on = oc * lax.rsqrt(var + EPS) * gng_ref[:, hd * DV:(hd + 1) * DV]
        og = (on * _silu(g)).astype(BF16)
        acc = acc + jnp.dot(og, wout_ref[hd * DV:(hd + 1) * DV, :], preferred_element_type=F32)
    y_ref[0] = x + gate * acc


def _ret_layer(x, mod, cos, sin, norm_g, w_in, gn_g, w_out, r0, *, T, C, H, DK, DV):
    B, L, D = x.shape
    has_state = r0 is not None
    kern = functools.partial(_ret_kernel, T=T, C=C, H=H, DK=DK, DV=DV, D=D, has_state=has_state)
    in_specs = [
        pl.BlockSpec((1, T, D), lambda b, t: (b, t, 0)),
        pl.BlockSpec((1, 1, 3 * D), lambda b, t: (b, 0, 0)),
        pl.BlockSpec((T, DK // 2), lambda b, t: (t, 0)),
        pl.BlockSpec((T, DK // 2), lambda b, t: (t, 0)),
        _const_spec((1, D)),
        _const_spec(w_in.shape),
        _const_spec((1, H * DV)),
        _const_spec(w_out.shape),
    ]
    args = [x, mod.reshape(B, 1, 3 * D), cos, sin, norm_g.reshape(1, D), w_in,
            gn_g.reshape(1, H * DV), w_out]
    if has_state:
        in_specs.append(pl.BlockSpec((1, H, DK, DV), lambda b, t: (b, 0, 0, 0)))
        args.append(r0)
    return pl.pallas_call(
        kern,
        out_shape=(jax.ShapeDtypeStruct((B, L, D), F32),
                   jax.ShapeDtypeStruct((B, H, DK, DV), F32)),
        grid=(B, L // T),
        in_specs=in_specs,
        out_specs=(pl.BlockSpec((1, T, D), lambda b, t: (b, t, 0)),
                   pl.BlockSpec((1, H, DK, DV), lambda b, t: (b, 0, 0, 0))),
        compiler_params=pltpu.CompilerParams(dimension_semantics=("arbitrary", "arbitrary"),
                                             vmem_limit_bytes=VMEM_LIMIT),
        name="ret_layer",
    )(*args)


def _store_kv_wide(ka_ref, va_ref, k_heads, v, fc, place_ref, *, HKV, DH):
    rows = v.shape[0]
    bias = _key_bias_cols(fc, place_ref)
    one = _lane_onehot(rows, 0, 1)
    w = DH + LANES
    for hd in range(HKV):
        ka_ref[0, :, hd * w:hd * w + DH] = k_heads[hd].astype(BF16)
        ka_ref[0, :, hd * w + DH:(hd + 1) * w] = bias[:, hd * LANES:(hd + 1) * LANES]
        va_ref[0, :, hd * w:hd * w + DH] = v[:, hd * DH:(hd + 1) * DH].astype(BF16)
        va_ref[0, :, hd * w + DH:(hd + 1) * w] = one


def _past_kernel(k_ref, v_ref, lf_ref, place_ref, ka_ref, va_ref, fc_ref, carry_ref, *, HKV, DH):
    @pl.when(pl.program_id(1) == 0)
    def _():
        carry_ref[...] = jnp.zeros_like(carry_ref)

    fc = _tri_cumsum(lf_ref[0]) + carry_ref[...]
    carry_ref[...] = fc[-1:, :]
    fc_ref[0] = fc[-1:, :]
    k_heads = [k_ref[0, :, hd * DH:(hd + 1) * DH] for hd in range(HKV)]
    _store_kv_wide(ka_ref, va_ref, k_heads, v_ref[0], fc, place_ref, HKV=HKV, DH=DH)


def _past(k, v, lf, place, *, T, HKV, DH):
    B, S, kv_w = k.shape
    wide = HKV * (DH + LANES)
    tok = lambda w: pl.BlockSpec((1, T, w), lambda b, t: (b, t, 0))
    return pl.pallas_call(
        functools.partial(_past_kernel, HKV=HKV, DH=DH),
        out_shape=(jax.ShapeDtypeStruct((B, S, wide), BF16),
                   jax.ShapeDtypeStruct((B, S, wide), BF16),
                   jax.ShapeDtypeStruct((B, 1, LANES), F32)),
        grid=(B, S // T),
        in_specs=[tok(kv_w), tok(kv_w), tok(LANES), _const_spec(place.shape)],
        out_specs=(tok(wide), tok(wide), pl.BlockSpec((1, 1, LANES), lambda b, t: (b, 0, 0))),
        scratch_shapes=[pltpu.VMEM((1, LANES), F32)],
        compiler_params=pltpu.CompilerParams(dimension_semantics=("arbitrary", "arbitrary")),
        name="past_kv",
    )(k, v, lf, place)


def _prep_kernel(x_ref, kvmod_ref, bmod_ref, f0_ref, kvng_ref, bng_ref, kvw_ref, fw_ref, fb_ref,
                 kng_ref, bwin_ref, qng_ref, place_ref,
                 k_ref, v_ref, lf_ref, ka_ref, va_ref, qa_ref, gb_ref,
                 carry_ref, *, T, D, HKV, G, DH):
    @pl.when(pl.program_id(1) == 0)
    def _():
        carry_ref[...] = f0_ref[0]

    x = x_ref[0]
    kvmod = kvmod_ref[0]
    hk = _modulated_norm(x, kvng_ref[...], kvmod[:, :D], kvmod[:, D:]).astype(BF16)
    kv_w = HKV * DH
    kng = kng_ref[...]
    k_heads = []
    for hd in range(HKV):
        kh = jnp.dot(hk, kvw_ref[:, hd * DH:(hd + 1) * DH], preferred_element_type=F32)
        kh = kh * lax.rsqrt(jnp.mean(kh * kh, axis=-1, keepdims=True) + EPS) * kng
        k_ref[0, :, hd * DH:(hd + 1) * DH] = kh
        k_heads.append(kh)
    v = jnp.dot(hk, kvw_ref[:, kv_w:], preferred_element_type=F32)
    v_ref[0] = v
    f = jnp.dot(hk, fw_ref[...], preferred_element_type=F32)
    lf = _log_sigmoid(f + fb_ref[...])
    lf_ref[0] = lf
    fc = _tri_cumsum(lf) + carry_ref[...]
    carry_ref[...] = fc[-1:, :]
    _store_kv_wide(ka_ref, va_ref, k_heads, v, fc, place_ref, HKV=HKV, DH=DH)

    bmod = bmod_ref[0]
    hq = _modulated_norm(x, bng_ref[...], bmod[:, :D], bmod[:, D:2 * D]).astype(BF16)
    qng = qng_ref[...] * (DH ** -0.5 * LOG2E)
    fox_w = HKV * G * DH
    for kvh in range(HKV):
        for g in range(G):
            hd = kvh * G + g
            qh = jnp.dot(hq, bwin_ref[:, hd * DH:(hd + 1) * DH], preferred_element_type=F32)
            qh = qh * lax.rsqrt(jnp.mean(qh * qh, axis=-1, keepdims=True) + EPS) * qng
            qa_ref[0, 0, kvh, g * T:(g + 1) * T, :DH] = qh.astype(BF16)
            qa_ref[0, 0, kvh, g * T:(g + 1) * T, DH:] = _lane_onehot(T, 3 * g, 3 * g + 3)
    gg = jnp.dot(hq, bwin_ref[:, fox_w:], preferred_element_type=F32)
    gb_ref[0] = _silu(gg).astype(BF16)


def _prep(x1, kvmod, bmod, f0, kv_norm_g, b_norm_g, kvw, fw, fb, k_norm_g, b_w_in, q_norm_g,
          place, *, T, HKV, HQ, DH):
    B, L, D = x1.shape
    G = HQ // HKV
    kv_w = HKV * DH
    fox_w = HQ * DH
    wide = HKV * (DH + LANES)
    kern = functools.partial(_prep_kernel, T=T, D=D, HKV=HKV, G=G, DH=DH)
    tok = lambda w: pl.BlockSpec((1, T, w), lambda b, t: (b, t, 0))
    per_b = lambda w: pl.BlockSpec((1, 1, w), lambda b, t: (b, 0, 0))
    return pl.pallas_call(
        kern,
        out_shape=(jax.ShapeDtypeStruct((B, L, kv_w), F32),
                   jax.ShapeDtypeStruct((B, L, kv_w), F32),
                   jax.ShapeDtypeStruct((B, L, LANES), F32),
                   jax.ShapeDtypeStruct((B, L, wide), BF16),
                   jax.ShapeDtypeStruct((B, L, wide), BF16),
                   jax.ShapeDtypeStruct((B, L // T, HKV, G * T, DH + LANES), BF16),
                   jax.ShapeDtypeStruct((B, L, fox_w), BF16)),
        grid=(B, L // T),
        in_specs=[tok(D), per_b(2 * D), per_b(3 * D), per_b(LANES),
                  _const_spec((1, D)), _const_spec((1, D)),
                  _const_spec(kvw.shape), _const_spec(fw.shape), _const_spec((1, LANES)),
                  _const_spec((1, DH)), _const_spec(b_w_in.shape), _const_spec((1, DH)),
                  _const_spec(place.shape)],
        out_specs=(tok(kv_w), tok(kv_w), tok(LANES), tok(wide), tok(wide),
                   pl.BlockSpec((1, 1, HKV, G * T, DH + LANES), lambda b, t: (b, t, 0, 0, 0)),
                   tok(fox_w)),
        scratch_shapes=[pltpu.VMEM((1, LANES), F32)],
        compiler_params=pltpu.CompilerParams(dimension_semantics=("arbitrary", "arbitrary"),
                                             vmem_limit_bytes=VMEM_LIMIT),
        name="prep",
    )(x1, kvmod.reshape(B, 1, 2 * D), bmod.reshape(B, 1, 3 * D), f0, kv_norm_g.reshape(1, D),
      b_norm_g.reshape(1, D), kvw, fw, fb, k_norm_g.reshape(1, DH), b_w_in,
      q_norm_g.reshape(1, DH), place)


def _attn_kernel(qi_ref, kj_ref, first_ref, last_ref, mask_ref,
                 q_ref, k_ref, v_ref, o_ref, m_ref, acc_ref, *, tq, tk, HKV, G, DH, q_off):
    s_id = pl.program_id(1)
    w = DH + LANES

    @pl.when(first_ref[s_id] == 1)
    def _():
        m_ref[...] = jnp.full_like(m_ref, M_INIT)
        acc_ref[...] = jnp.zeros_like(acc_ref)

    def step(masked):
        if masked:
            qpos = q_off + qi_ref[s_id] * tq + lax.broadcasted_iota(jnp.int32, (tq, tk), 0)
            kpos = kj_ref[s_id] * tk + lax.broadcasted_iota(jnp.int32, (tq, tk), 1)
            keep = jnp.concatenate([kpos <= qpos] * G, axis=0)
        for kvh in range(HKV):
            s = lax.dot_general(q_ref[0, 0, kvh], k_ref[0, :, kvh * w:(kvh + 1) * w], _NT,
                                preferred_element_type=F32)
            if masked:
                s = jnp.where(keep, s, -jnp.inf)
            chunks = [s[:, c * LANES:(c + 1) * LANES] for c in range(tk // LANES)]
            mx = functools.reduce(jnp.maximum, chunks)
            m_prev = m_ref[kvh]
            m_new = jnp.maximum(m_prev, jnp.max(mx, axis=-1, keepdims=True))
            alpha = jnp.exp2(m_prev - m_new)
            m_ref[kvh] = m_new
            p = jnp.concatenate([jnp.exp2(c - m_new).astype(BF16) for c in chunks], axis=-1)
            pv = jnp.dot(p, v_ref[0, :, kvh * w:(kvh + 1) * w], preferred_element_type=F32)
            acc_ref[kvh] = jnp.concatenate([alpha] * (w // LANES), axis=-1) * acc_ref[kvh] + pv

    @pl.when(mask_ref[s_id] == 1)
    def _():
        step(True)

    @pl.when(mask_ref[s_id] == 0)
    def _():
        step(False)

    @pl.when(last_ref[s_id] == 1)
    def _():
        for kvh in range(HKV):
            for g in range(G):
                a = acc_ref[kvh, g * tq:(g + 1) * tq, :]
                hd = kvh * G + g
                o_ref[0, :, hd * DH:(hd + 1) * DH] = (a[:, :DH] / a[:, DH:DH + 1]).astype(o_ref.dtype)


def _attn_tables(L, S, tq, tk, q_off):
    qi, kj, first, last, mask = [], [], [], [], []
    for i in range(L // tq):
        q_lo, q_hi = q_off + i * tq, q_off + (i + 1) * tq - 1
        js = [j for j in range(S // tk) if j * tk <= q_hi]
        for n, j in enumerate(js):
            qi.append(i)
            kj.append(j)
            first.append(int(n == 0))
            last.append(int(n == len(js) - 1))
            mask.append(int((j + 1) * tk - 1 > q_lo))
    return [jnp.asarray(np.asarray(a, np.int32)) for a in (qi, kj, first, last, mask)]


def _attn(qa, ka, va, *, tk, HKV, G, DH, q_off):
    B, nq, _, gtq, w = qa.shape
    tq = gtq // G
    L = nq * tq
    S = ka.shape[1]
    fox_w = HKV * G * DH
    tables = _attn_tables(L, S, tq, tk, q_off)
    n_steps = int(tables[0].shape[0])
    kern = functools.partial(_attn_kernel, tq=tq, tk=tk, HKV=HKV, G=G, DH=DH, q_off=q_off)
    grid_spec = pltpu.PrefetchScalarGridSpec(
        num_scalar_prefetch=5,
        grid=(B, n_steps),
        in_specs=[
            pl.BlockSpec((1, 1, HKV, gtq, w), lambda b, s, qi, kj, *_: (b, qi[s], 0, 0, 0)),
            pl.BlockSpec((1, tk, HKV * w), lambda b, s, qi, kj, *_: (b, kj[s], 0)),
            pl.BlockSpec((1, tk, HKV * w), lambda b, s, qi, kj, *_: (b, kj[s], 0)),
        ],
        out_specs=pl.BlockSpec((1, tq, fox_w), lambda b, s, qi, kj, *_: (b, qi[s], 0)),
        scratch_shapes=[pltpu.VMEM((HKV, gtq, LANES), F32), pltpu.VMEM((HKV, gtq, w), F32)],
    )
    return pl.pallas_call(
        kern,
        out_shape=jax.ShapeDtypeStruct((B, L, fox_w), BF16),
        grid_spec=grid_spec,
        compiler_params=pltpu.CompilerParams(dimension_semantics=("arbitrary", "arbitrary"),
                                             vmem_limit_bytes=VMEM_LIMIT),
        name="fox_attn",
    )(*tables, qa, ka, va)


def _fout_kernel(x_ref, o_ref, g_ref, mod_ref, w_ref, y_ref, *, D):
    og = (o_ref[0].astype(F32) * g_ref[0].astype(F32)).astype(BF16)
    gate = mod_ref[0][:, 2 * D:]
    y_ref[0] = x_ref[0] + gate * jnp.dot(og, w_ref[...], preferred_element_type=F32)


def _fout(x1, o, gb, bmod, w_out, *, T):
    B, L, D = x1.shape
    W = o.shape[-1]
    return pl.pallas_call(
        functools.partial(_fout_kernel, D=D),
        out_shape=jax.ShapeDtypeStruct((B, L, D), F32),
        grid=(B, L // T),
        in_specs=[pl.BlockSpec((1, T, D), lambda b, t: (b, t, 0)),
                  pl.BlockSpec((1, T, W), lambda b, t: (b, t, 0)),
                  pl.BlockSpec((1, T, W), lambda b, t: (b, t, 0)),
                  pl.BlockSpec((1, 1, 3 * D), lambda b, t: (b, 0, 0)),
                  _const_spec(w_out.shape)],
        out_specs=pl.BlockSpec((1, T, D), lambda b, t: (b, t, 0)),
        compiler_params=pltpu.CompilerParams(dimension_semantics=("arbitrary", "arbitrary")),
        name="fox_out",
    )(x1, o, gb, bmod.reshape(B, 1, 3 * D), w_out)


def _rope_tables(pos0, L, half):
    inv = jnp.power(ROPE_BASE, -jnp.arange(half, dtype=F32) / half)
    ang = (pos0 + jnp.arange(L)).astype(F32)[:, None] * inv[None, :]
    return jnp.cos(ang), jnp.sin(ang)


def _pad_lanes(a):
    return jnp.pad(a, [(0, 0)] * (a.ndim - 1) + [(0, LANES - a.shape[-1])])


def _trunk(x, a_mod, kv_mod, b_mod, pos0, r0, past, wts, dims, tiles):
    H, DK, DV, HKV, HQ, DH = dims
    G = HQ // HKV
    B, L, D = x.shape
    cos, sin = _rope_tables(pos0, L, DK // 2)
    x1, r = _ret_layer(x, a_mod, cos, sin, wts["a_norm_g"], wts["a_w_in"], wts["a_gn_g"],
                       wts["a_w_out"], r0, T=tiles["ret_t"], C=tiles["ret_c"],
                       H=H, DK=DK, DV=DV)
    if past is None:
        f0 = jnp.zeros((B, 1, LANES), F32)
    else:
        k_past, v_past, lf_past = past
        kv_w = HKV * DH
        ka_past, va_past, f0 = _past(k_past.reshape(B, -1, kv_w), v_past.reshape(B, -1, kv_w),
                                     _pad_lanes(lf_past.astype(F32)), wts["place"],
                                     T=tiles["past_t"], HKV=HKV, DH=DH)
    k, v, lf, ka, va, qa, gb = _prep(
        x1, kv_mod, b_mod, f0, wts["kv_norm_g"], wts["b_norm_g"], wts["kvw"], wts["fw"],
        wts["fb"], wts["k_norm_g"], wts["b_w_in"], wts["b_q_norm_g"], wts["place"],
        T=tiles["attn_tq"], HKV=HKV, HQ=HQ, DH=DH)
    q_off = 0
    if past is not None:
        q_off = k_past.shape[1]
        pad = (-(q_off + L)) % tiles["attn_tk"]
        cat = lambda p, n: jnp.pad(jnp.concatenate([p, n], axis=1), ((0, 0), (0, pad), (0, 0)))
        ka, va = cat(ka_past, ka), cat(va_past, va)
    o = _attn(qa, ka, va, tk=tiles["attn_tk"], HKV=HKV, G=G, DH=DH, q_off=q_off)
    y = _fout(x1, o, gb, b_mod, wts["b_w_out"], T=tiles["attn_tq"])
    return (y, r[None], k.reshape(B, L, HKV, DH), v.reshape(B, L, HKV, DH), lf[:, :, :HQ])


def kernel(x_prompt, x_sample, state_ret, cache_k, cache_v, cache_logf, c_prompt, c_sample, a_norm_g, a_ada_w, a_ada_b, a_w_in, a_gn_g, a_w_out, kv_norm_g, kv_ada_w, kv_ada_b, kv_w, kv_fb, k_norm_g, b_norm_g, b_ada_w, b_ada_b, b_w_in, b_q_norm_g, b_w_out):
    assert a_w_in.shape[0] == 1 and b_w_in.shape[0] == 1, "one retention and one attention layer"
    Bp, Lp, D = x_prompt.shape
    Bs, Ls, _ = x_sample.shape
    _, _, H, DK, DV = state_ret.shape
    _, past_len, HKV, DH = cache_k.shape
    HQ = cache_logf.shape[-1]
    assert DH == LANES and 3 * (HQ // HKV) <= LANES
    dims = (H, DK, DV, HKV, HQ, DH)
    kv_wd = HKV * DH

    c_all = jnp.concatenate([c_prompt, c_sample], axis=0)
    a_mod = _ada(c_all, a_ada_w[0], a_ada_b[0])
    kv_mod = _ada(c_all, kv_ada_w, kv_ada_b)
    b_mod = _ada(c_all, b_ada_w[0], b_ada_b[0])

    wts = dict(
        a_norm_g=a_norm_g[0], a_w_in=a_w_in[0].astype(BF16), a_gn_g=a_gn_g[0],
        a_w_out=a_w_out[0].astype(BF16), kv_norm_g=kv_norm_g,
        kvw=kv_w[:, :2 * kv_wd].astype(BF16), fw=_pad_lanes(kv_w[:, 2 * kv_wd:]).astype(BF16),
        fb=_pad_lanes(kv_fb.reshape(1, HQ)), k_norm_g=k_norm_g, b_norm_g=b_norm_g[0],
        b_w_in=b_w_in[0].astype(BF16), b_q_norm_g=b_q_norm_g[0], b_w_out=b_w_out[0].astype(BF16),
        place=_place_matrix(HKV, HQ // HKV))

    p_tiles = dict(ret_t=min(512, Lp), ret_c=min(256, Lp), attn_tq=min(512, Lp),
                   attn_tk=min(512, Lp))
    yp, rp, kp, vp, lfp = _trunk(x_prompt, a_mod[:Bp], kv_mod[:Bp], b_mod[:Bp], 0, None, None,
                                 wts, dims, p_tiles)
    s_total = past_len + Ls
    s_tiles = dict(ret_t=Ls, ret_c=Ls, attn_tq=Ls, attn_tk=-(-s_total // LANES) * LANES,
                   past_t=min(512, past_len))
    ys, rs, ks, vs, lfs = _trunk(x_sample, a_mod[Bp:], kv_mod[Bp:], b_mod[Bp:], past_len,
                                 state_ret[0], (cache_k, cache_v, cache_logf), wts, dims, s_tiles)
    return (yp, ys, rp, kp, vp, lfp, rs, ks, vs, lfs)
```

```python
import functools
import math

import numpy as np
import jax
import jax.numpy as jnp
from jax import lax
from jax.experimental import pallas as pl
from jax.experimental.pallas import tpu as pltpu

F32 = jnp.float32
BF16 = jnp.bfloat16
EPS = 1e-6
ROPE_BASE = 10000.0
LOG2E = 1.4426950408889634
LANES = 128
VMEM_LIMIT = 56 * 1024 * 1024
M_INIT = -1e30

_NT = (((1,), (1,)), ((), ()))
_TN = (((0,), (0,)), ((), ()))


def _const_spec(shape):
    return pl.BlockSpec(shape, lambda *_: (0,) * len(shape), pipeline_mode=pl.Buffered(1))


def _silu(x):
    return x / (1.0 + jnp.exp(-x))


def _log_sigmoid(x):
    return jnp.minimum(x, 0.0) - jnp.log(1.0 + jnp.exp(-jnp.abs(x)))


def _modulated_norm(x, g, shift, scale):
    xh = x * lax.rsqrt(jnp.mean(x * x, axis=-1, keepdims=True) + EPS)
    return (xh * g) * (1.0 + scale) + shift


def _split3(x):
    hi = x.astype(BF16)
    r1 = x - hi.astype(F32)
    mid = r1.astype(BF16)
    lo = (r1 - mid.astype(F32)).astype(BF16)
    return hi, mid, lo


def _tri_cumsum(lf):
    t = lf.shape[0]
    row = lax.broadcasted_iota(jnp.int32, (t, t), 0)
    col = lax.broadcasted_iota(jnp.int32, (t, t), 1)
    tri = jnp.where(col <= row, 1.0, 0.0).astype(BF16)
    hi, mid, lo = _split3(lf)
    out = jnp.dot(tri, hi, preferred_element_type=F32)
    out = out + jnp.dot(tri, mid, preferred_element_type=F32)
    return out + jnp.dot(tri, lo, preferred_element_type=F32)


def _lane_onehot(rows, lo, hi):
    lane = lax.broadcasted_iota(jnp.int32, (rows, LANES), 1)
    return jnp.where((lane >= lo) & (lane < hi), 1.0, 0.0).astype(BF16)


def _key_bias_cols(fc, place_ref):
    hi, mid, lo = _split3(fc * (-LOG2E))
    cat = jnp.concatenate([hi, mid, lo], axis=-1)
    return jnp.dot(cat, place_ref[...], preferred_element_type=F32).astype(BF16)


def _place_matrix(HKV, G):
    p = np.zeros((3 * LANES, HKV * LANES), np.float32)
    for kvh in range(HKV):
        for g in range(G):
            for j in range(3):
                p[j * LANES + kvh * G + g, kvh * LANES + 3 * g + j] = 1.0
    return jnp.asarray(p, BF16)


def _ada_kernel(c_ref, w_ref, b_ref, o_ref):
    s = _silu(c_ref[...]).astype(BF16)
    o_ref[...] = jnp.dot(s, w_ref[...].astype(BF16), preferred_element_type=F32) + b_ref[...]


def _ada(c, w, b, *, tn=1024):
    m, d = c.shape
    n = w.shape[1]
    return pl.pallas_call(
        _ada_kernel,
        out_shape=jax.ShapeDtypeStruct((m, n), F32),
        grid=(n // tn,),
        in_specs=[pl.BlockSpec((m, d), lambda j: (0, 0)),
                  pl.BlockSpec((d, tn), lambda j: (0, j)),
                  pl.BlockSpec((1, tn), lambda j: (0, j))],
        out_specs=pl.BlockSpec((m, tn), lambda j: (0, j)),
        compiler_params=pltpu.CompilerParams(dimension_semantics=("arbitrary",)),
        name="ada",
    )(c, w, b.reshape(1, n))


def _ret_kernel(*refs, T, C, H, DK, DV, D, has_state):
    if has_state:
        (x_ref, mod_ref, cos_ref, sin_ref, ng_ref, win_ref, gng_ref, wout_ref, r0_ref,
         y_ref, r_ref) = refs
    else:
        (x_ref, mod_ref, cos_ref, sin_ref, ng_ref, win_ref, gng_ref, wout_ref,
         y_ref, r_ref) = refs
        r0_ref = None

    @pl.when(pl.program_id(1) == 0)
    def _():
        if has_state:
            r_ref[...] = r0_ref[...]
        else:
            r_ref[...] = jnp.zeros_like(r_ref)

    x = x_ref[0]
    mod = mod_ref[0]
    shift, scale, gate = mod[:, :D], mod[:, D:2 * D], mod[:, 2 * D:]
    h = _modulated_norm(x, ng_ref[...], shift, scale).astype(BF16)
    cos = cos_ref[...]
    sin = sin_ref[...]
    half = DK // 2
    qk_w = H * DK

    row = lax.broadcasted_iota(jnp.int32, (C, C), 0)
    col = lax.broadcasted_iota(jnp.int32, (C, C), 1)
    diff = row - col
    diff_f = jnp.maximum(diff, 0).astype(F32)
    idx = lax.broadcasted_iota(jnp.int32, (C, 1), 0).astype(F32)

    def rot(t):
        t1, t2 = t[:, :half], t[:, half:]
        return jnp.concatenate([t1 * cos - t2 * sin, t2 * cos + t1 * sin], axis=-1)

    ogs = []
    for hd in range(H):
        lg = math.log1p(-(2.0 ** (-5.0 - hd)))
        q = rot(jnp.dot(h, win_ref[:, hd * DK:(hd + 1) * DK], preferred_element_type=F32))
        k = rot(jnp.dot(h, win_ref[:, qk_w + hd * DK:qk_w + (hd + 1) * DK],
                        preferred_element_type=F32)) * (DK ** -0.5)
        v = jnp.dot(h, win_ref[:, 2 * qk_w + hd * DV:2 * qk_w + (hd + 1) * DV],
                    preferred_element_type=F32).astype(BF16)
        g = jnp.dot(h, win_ref[:, 2 * qk_w + H * DV + hd * DV:2 * qk_w + H * DV + (hd + 1) * DV],
                    preferred_element_type=F32)
        qb = q.astype(BF16)
        kb = k.astype(BF16)
        intra = jnp.where(diff >= 0, jnp.exp(diff_f * lg), 0.0)
        q_dec = jnp.exp((idx + 1.0) * lg)
        k_dec = jnp.exp((C - 1.0 - idx) * lg)
        c_dec = math.exp(C * lg)
        outs = []
        for c in range(T // C):
            sl = slice(c * C, (c + 1) * C)
            r = r_ref[0, hd]
            s = lax.dot_general(qb[sl], kb[sl], _NT, preferred_element_type=F32) * intra
            o = jnp.dot(s.astype(BF16), v[sl], preferred_element_type=F32)
            o = o + jnp.dot(qb[sl], r.astype(BF16), preferred_element_type=F32) * q_dec
            kd = (k[sl] * k_dec).astype(BF16)
            r_ref[0, hd] = r * c_dec + lax.dot_general(kd, v[sl], _TN, preferred_element_type=F32)
            outs.append(o)
        o = outs[0] if len(outs) == 1 else jnp.concatenate(outs, axis=0)
        mu = jnp.mean(o, axis=-1, keepdims=True)
        oc = o - mu
        var = jnp.mean(oc * oc, axis=-1, keepdims=True)
        on = oc * lax.rsqrt(var + EPS) * gng_ref[:, hd * DV:(hd + 1) * DV]
        ogs.append((on * _silu(g)).astype(BF16))
    proj = jnp.dot(jnp.concatenate(ogs, axis=-1), wout_ref[...], preferred_element_type=F32)
    y_ref[0] = x + gate * proj


def _ret_layer(x, mod, cos, sin, norm_g, w_in, gn_g, w_out, r0, *, T, C, H, DK, DV):
    B, L, D = x.shape
    has_state = r0 is not None
    kern = functools.partial(_ret_kernel, T=T, C=C, H=H, DK=DK, DV=DV, D=D, has_state=has_state)
    in_specs = [
        pl.BlockSpec((1, T, D), lambda b, t: (b, t, 0)),
        pl.BlockSpec((1, 1, 3 * D), lambda b, t: (b, 0, 0)),
        pl.BlockSpec((T, DK // 2), lambda b, t: (t, 0)),
        pl.BlockSpec((T, DK // 2), lambda b, t: (t, 0)),
        _const_spec((1, D)),
        _const_spec(w_in.shape),
        _const_spec((1, H * DV)),
        _const_spec(w_out.shape),
    ]
    args = [x, mod.reshape(B, 1, 3 * D), cos, sin, norm_g.reshape(1, D), w_in,
            gn_g.reshape(1, H * DV), w_out]
    if has_state:
        in_specs.append(pl.BlockSpec((1, H, DK, DV), lambda b, t: (b, 0, 0, 0)))
        args.append(r0)
    return pl.pallas_call(
        kern,
        out_shape=(jax.ShapeDtypeStruct((B, L, D), F32),
                   jax.ShapeDtypeStruct((B, H, DK, DV), F32)),
        grid=(B, L // T),
        in_specs=in_specs,
        out_specs=(pl.BlockSpec((1, T, D), lambda b, t: (b, t, 0)),
                   pl.BlockSpec((1, H, DK, DV), lambda b, t: (b, 0, 0, 0))),
        compiler_params=pltpu.CompilerParams(dimension_semantics=("arbitrary", "arbitrary"),
                                             vmem_limit_bytes=VMEM_LIMIT),
        name="ret_layer",
    )(*args)


def _store_kv_wide(ka_ref, va_ref, k_heads, v, fc, place_ref, *, HKV, DH):
    rows = v.shape[0]
    bias = _key_bias_cols(fc, place_ref)
    one = _lane_onehot(rows, 0, 1)
    w = DH + LANES
    for hd in range(HKV):
        ka_ref[0, :, hd * w:hd * w + DH] = k_heads[hd].astype(BF16)
        ka_ref[0, :, hd * w + DH:(hd + 1) * w] = bias[:, hd * LANES:(hd + 1) * LANES]
        va_ref[0, :, hd * w:hd * w + DH] = v[:, hd * DH:(hd + 1) * DH].astype(BF16)
        va_ref[0, :, hd * w + DH:(hd + 1) * w] = one


def _past_kernel(k_ref, v_ref, lf_ref, place_ref, ka_ref, va_ref, fc_ref, carry_ref, *, HKV, DH):
    @pl.when(pl.program_id(1) == 0)
    def _():
        carry_ref[...] = jnp.zeros_like(carry_ref)

    fc = _tri_cumsum(lf_ref[0]) + carry_ref[...]
    carry_ref[...] = fc[-1:, :]
    fc_ref[0] = fc[-1:, :]
    rows = k_ref.shape[1]
    k = k_ref[0].reshape(rows, HKV * DH)
    k_heads = [k[:, hd * DH:(hd + 1) * DH] for hd in range(HKV)]
    _store_kv_wide(ka_ref, va_ref, k_heads, v_ref[0].reshape(rows, HKV * DH), fc, place_ref,
                   HKV=HKV, DH=DH)


def _past(k, v, lf, place, *, T, HKV, DH):
    B, S = k.shape[:2]
    wide = HKV * (DH + LANES)
    tok = lambda w: pl.BlockSpec((1, T, w), lambda b, t: (b, t, 0))
    heads = pl.BlockSpec((1, T, HKV, DH), lambda b, t: (b, t, 0, 0))
    return pl.pallas_call(
        functools.partial(_past_kernel, HKV=HKV, DH=DH),
        out_shape=(jax.ShapeDtypeStruct((B, S, wide), BF16),
                   jax.ShapeDtypeStruct((B, S, wide), BF16),
                   jax.ShapeDtypeStruct((B, 1, LANES), F32)),
        grid=(B, S // T),
        in_specs=[heads, heads, tok(LANES), _const_spec(place.shape)],
        out_specs=(tok(wide), tok(wide), pl.BlockSpec((1, 1, LANES), lambda b, t: (b, 0, 0))),
        scratch_shapes=[pltpu.VMEM((1, LANES), F32)],
        compiler_params=pltpu.CompilerParams(dimension_semantics=("arbitrary", "arbitrary")),
        name="past_kv",
    )(k, v, lf, place)


def _prep_kernel(x_ref, kvmod_ref, bmod_ref, f0_ref, kvng_ref, bng_ref, kvw_ref, fw_ref, fb_ref,
                 kng_ref, bwin_ref, qng_ref, place_ref,
                 k_ref, v_ref, lf_ref, ka_ref, va_ref, qa_ref, gb_ref,
                 carry_ref, *, T, D, HKV, G, DH):
    @pl.when(pl.program_id(1) == 0)
    def _():
        carry_ref[...] = f0_ref[0]

    x = x_ref[0]
    kvmod = kvmod_ref[0]
    hk = _modulated_norm(x, kvng_ref[...], kvmod[:, :D], kvmod[:, D:]).astype(BF16)
    kv_w = HKV * DH
    kng = kng_ref[...]
    k_raw = jnp.dot(hk, kvw_ref[:, :kv_w], preferred_element_type=F32)
    k_heads = []
    for hd in range(HKV):
        kh = k_raw[:, hd * DH:(hd + 1) * DH]
        k_heads.append(kh * lax.rsqrt(jnp.mean(kh * kh, axis=-1, keepdims=True) + EPS) * kng)
    k_ref[0] = jnp.concatenate(k_heads, axis=-1).reshape(T, HKV, DH)
    v = jnp.dot(hk, kvw_ref[:, kv_w:], preferred_element_type=F32)
    v_ref[0] = v.reshape(T, HKV, DH)
    f = jnp.dot(hk, fw_ref[...], preferred_element_type=F32)
    lf = _log_sigmoid(f + fb_ref[...])
    lf_ref[0] = lf
    fc = _tri_cumsum(lf) + carry_ref[...]
    carry_ref[...] = fc[-1:, :]
    _store_kv_wide(ka_ref, va_ref, k_heads, v, fc, place_ref, HKV=HKV, DH=DH)

    bmod = bmod_ref[0]
    hq = _modulated_norm(x, bng_ref[...], bmod[:, :D], bmod[:, D:2 * D]).astype(BF16)
    qng = qng_ref[...] * (DH ** -0.5 * LOG2E)
    fox_w = HKV * G * DH
    q_raw = jnp.dot(hq, bwin_ref[:, :fox_w], preferred_element_type=F32)
    for kvh in range(HKV):
        for g in range(G):
            hd = kvh * G + g
            qh = q_raw[:, hd * DH:(hd + 1) * DH]
            qh = qh * lax.rsqrt(jnp.mean(qh * qh, axis=-1, keepdims=True) + EPS) * qng
            qa_ref[0, 0, kvh, g * T:(g + 1) * T, :DH] = qh.astype(BF16)
            qa_ref[0, 0, kvh, g * T:(g + 1) * T, DH:] = _lane_onehot(T, 3 * g, 3 * g + 3)
    gg = jnp.dot(hq, bwin_ref[:, fox_w:], preferred_element_type=F32)
    gb_ref[0] = _silu(gg).astype(BF16)


def _prep(x1, kvmod, bmod, f0, kv_norm_g, b_norm_g, kvw, fw, fb, k_norm_g, b_w_in, q_norm_g,
          place, *, T, HKV, HQ, DH):
    B, L, D = x1.shape
    G = HQ // HKV
    kv_w = HKV * DH
    fox_w = HQ * DH
    wide = HKV * (DH + LANES)
    kern = functools.partial(_prep_kernel, T=T, D=D, HKV=HKV, G=G, DH=DH)
    tok = lambda w: pl.BlockSpec((1, T, w), lambda b, t: (b, t, 0))
    per_b = lambda w: pl.BlockSpec((1, 1, w), lambda b, t: (b, 0, 0))
    return pl.pallas_call(
        kern,
        out_shape=(jax.ShapeDtypeStruct((B, L, HKV, DH), F32),
                   jax.ShapeDtypeStruct((B, L, HKV, DH), F32),
                   jax.ShapeDtypeStruct((B, L, LANES), F32),
                   jax.ShapeDtypeStruct((B, L, wide), BF16),
                   jax.ShapeDtypeStruct((B, L, wide), BF16),
                   jax.ShapeDtypeStruct((B, L // T, HKV, G * T, DH + LANES), BF16),
                   jax.ShapeDtypeStruct((B, L, fox_w), BF16)),
        grid=(B, L // T),
        in_specs=[tok(D), per_b(2 * D), per_b(3 * D), per_b(LANES),
                  _const_spec((1, D)), _const_spec((1, D)),
                  _const_spec(kvw.shape), _const_spec(fw.shape), _const_spec((1, LANES)),
                  _const_spec((1, DH)), _const_spec(b_w_in.shape), _const_spec((1, DH)),
                  _const_spec(place.shape)],
        out_specs=(pl.BlockSpec((1, T, HKV, DH), lambda b, t: (b, t, 0, 0)),
                   pl.BlockSpec((1, T, HKV, DH), lambda b, t: (b, t, 0, 0)),
                   tok(LANES), tok(wide), tok(wide),
                   pl.BlockSpec((1, 1, HKV, G * T, DH + LANES), lambda b, t: (b, t, 0, 0, 0)),
                   tok(fox_w)),
        scratch_shapes=[pltpu.VMEM((1, LANES), F32)],
        compiler_params=pltpu.CompilerParams(dimension_semantics=("arbitrary", "arbitrary"),
                                             vmem_limit_bytes=VMEM_LIMIT),
        name="prep",
    )(x1, kvmod.reshape(B, 1, 2 * D), bmod.reshape(B, 1, 3 * D), f0, kv_norm_g.reshape(1, D),
      b_norm_g.reshape(1, D), kvw, fw, fb, k_norm_g.reshape(1, DH), b_w_in,
      q_norm_g.reshape(1, DH), place)


def _attn_kernel(lq_qi, lq_kj, lq_mask, lq_first, pv_first, pv_last, pv_qi, pv_kj,
                 q_ref, k_ref, v_ref, o_ref, m_ref, acc_ref, pa_ref, pb_ref, aa_ref, ab_ref,
                 *, n_pairs, tq, tk, HKV, G, DH, q_off):
    n = pl.program_id(1)
    w = DH + LANES

    def weights(p_ref, a_ref, masked):
        if masked:
            qpos = q_off + lq_qi[n] * tq + lax.broadcasted_iota(jnp.int32, (tq, tk), 0)
            kpos = lq_kj[n] * tk + lax.broadcasted_iota(jnp.int32, (tq, tk), 1)
            keep = jnp.concatenate([kpos <= qpos] * G, axis=0)
        for kvh in range(HKV):
            s = lax.dot_general(q_ref[0, 0, kvh], k_ref[0, :, kvh * w:(kvh + 1) * w], _NT,
                                preferred_element_type=F32)
            if masked:
                s = jnp.where(keep, s, -jnp.inf)
            chunks = [s[:, c * LANES:(c + 1) * LANES] for c in range(tk // LANES)]
            mx = functools.reduce(jnp.maximum, chunks)
            m_prev = m_ref[kvh]
            m_new = jnp.maximum(m_prev, jnp.max(mx, axis=-1, keepdims=True))
            m_ref[kvh] = m_new
            a_ref[kvh] = jnp.exp2(m_prev - m_new)
            for c, chunk in enumerate(chunks):
                p_ref[kvh, :, c * LANES:(c + 1) * LANES] = jnp.exp2(chunk - m_new).astype(BF16)

    def accumulate(p_ref, a_ref):
        for kvh in range(HKV):
            pv = jnp.dot(p_ref[kvh], v_ref[0, :, kvh * w:(kvh + 1) * w], preferred_element_type=F32)
            alpha = a_ref[kvh]
            acc_ref[kvh] = jnp.concatenate([alpha] * (w // LANES), axis=-1) * acc_ref[kvh] + pv

    @pl.when(lq_first[n] == 1)
    def _():
        m_ref[...] = jnp.full_like(m_ref, M_INIT)

    @pl.when(pv_first[n] == 1)
    def _():
        acc_ref[...] = jnp.zeros_like(acc_ref)

    p_bufs = (pa_ref, pb_ref)
    a_bufs = (aa_ref, ab_ref)
    middle = jnp.logical_and(n >= 1, n < n_pairs)
    parity = lax.rem(n, 2)
    for par in (0, 1):
        for masked in (False, True):
            @pl.when(middle & (parity == par) & (lq_mask[n] == int(masked)))
            def _(par=par, masked=masked):
                weights(p_bufs[par], a_bufs[par], masked)
                accumulate(p_bufs[1 - par], a_bufs[1 - par])

    @pl.when(n == 0)
    def _():
        weights(p_bufs[0], a_bufs[0], True)

    @pl.when(n == n_pairs)
    def _():
        accumulate(p_bufs[(n_pairs - 1) % 2], a_bufs[(n_pairs - 1) % 2])

    @pl.when(pv_last[n] == 1)
    def _():
        for kvh in range(HKV):
            for g in range(G):
                a = acc_ref[kvh, g * tq:(g + 1) * tq, :]
                hd = kvh * G + g
                o_ref[0, :, hd * DH:(hd + 1) * DH] = (a[:, :DH] / a[:, DH:DH + 1]).astype(o_ref.dtype)


def _attn_tables(L, S, tq, tk, q_off):
    qi, kj, first, last, mask = [], [], [], [], []
    for i in range(L // tq):
        q_lo, q_hi = q_off + i * tq, q_off + (i + 1) * tq - 1
        js = [j for j in range(S // tk) if j * tk <= q_hi]
        for n, j in enumerate(js):
            qi.append(i)
            kj.append(j)
            first.append(int(n == 0))
            last.append(int(n == len(js) - 1))
            mask.append(int((j + 1) * tk - 1 > q_lo))
    assert mask[0] == 1
    cols = (qi + qi[-1:], kj + kj[-1:], mask + [0], first + [0],
            [0] + first, [0] + last, qi[:1] + qi, kj[:1] + kj)
    return len(qi), [jnp.asarray(np.asarray(a, np.int32)) for a in cols]


def _attn(qa, ka, va, *, tk, HKV, G, DH, q_off):
    B, nq, _, gtq, w = qa.shape
    tq = gtq // G
    L = nq * tq
    S = ka.shape[1]
    fox_w = HKV * G * DH
    n_pairs, tables = _attn_tables(L, S, tq, tk, q_off)
    kern = functools.partial(_attn_kernel, n_pairs=n_pairs, tq=tq, tk=tk, HKV=HKV, G=G, DH=DH,
                             q_off=q_off)
    grid_spec = pltpu.PrefetchScalarGridSpec(
        num_scalar_prefetch=8,
        grid=(B, n_pairs + 1),
        in_specs=[
            pl.BlockSpec((1, 1, HKV, gtq, w), lambda b, n, *t: (b, t[0][n], 0, 0, 0)),
            pl.BlockSpec((1, tk, HKV * w), lambda b, n, *t: (b, t[1][n], 0)),
            pl.BlockSpec((1, tk, HKV * w), lambda b, n, *t: (b, t[7][n], 0)),
        ],
        out_specs=pl.BlockSpec((1, tq, fox_w), lambda b, n, *t: (b, t[6][n], 0)),
        scratch_shapes=[pltpu.VMEM((HKV, gtq, LANES), F32), pltpu.VMEM((HKV, gtq, w), F32),
                        pltpu.VMEM((HKV, gtq, tk), BF16), pltpu.VMEM((HKV, gtq, tk), BF16),
                        pltpu.VMEM((HKV, gtq, LANES), F32), pltpu.VMEM((HKV, gtq, LANES), F32)],
    )
    return pl.pallas_call(
        kern,
        out_shape=jax.ShapeDtypeStruct((B, L, fox_w), BF16),
        grid_spec=grid_spec,
        compiler_params=pltpu.CompilerParams(dimension_semantics=("arbitrary", "arbitrary"),
                                             vmem_limit_bytes=VMEM_LIMIT),
        name="fox_attn",
    )(*tables, qa, ka, va)


def _fout_kernel(x_ref, o_ref, g_ref, mod_ref, w_ref, y_ref, *, D):
    og = (o_ref[0].astype(F32) * g_ref[0].astype(F32)).astype(BF16)
    gate = mod_ref[0][:, 2 * D:]
    y_ref[0] = x_ref[0] + gate * jnp.dot(og, w_ref[...], preferred_element_type=F32)


def _fout(x1, o, gb, bmod, w_out, *, T):
    B, L, D = x1.shape
    W = o.shape[-1]
    return pl.pallas_call(
        functools.partial(_fout_kernel, D=D),
        out_shape=jax.ShapeDtypeStruct((B, L, D), F32),
        grid=(B, L // T),
        in_specs=[pl.BlockSpec((1, T, D), lambda b, t: (b, t, 0)),
                  pl.BlockSpec((1, T, W), lambda b, t: (b, t, 0)),
                  pl.BlockSpec((1, T, W), lambda b, t: (b, t, 0)),
                  pl.BlockSpec((1, 1, 3 * D), lambda b, t: (b, 0, 0)),
                  _const_spec(w_out.shape)],
        out_specs=pl.BlockSpec((1, T, D), lambda b, t: (b, t, 0)),
        compiler_params=pltpu.CompilerParams(dimension_semantics=("arbitrary", "arbitrary")),
        name="fox_out",
    )(x1, o, gb, bmod.reshape(B, 1, 3 * D), w_out)


def _rope_tables(pos0, L, half):
    inv = jnp.power(ROPE_BASE, -jnp.arange(half, dtype=F32) / half)
    ang = (pos0 + jnp.arange(L)).astype(F32)[:, None] * inv[None, :]
    return jnp.cos(ang), jnp.sin(ang)


def _pad_lanes(a):
    return jnp.pad(a, [(0, 0)] * (a.ndim - 1) + [(0, LANES - a.shape[-1])])


def _trunk(x, a_mod, kv_mod, b_mod, pos0, r0, past, wts, dims, tiles):
    H, DK, DV, HKV, HQ, DH = dims
    G = HQ // HKV
    B, L, D = x.shape
    cos, sin = _rope_tables(pos0, L, DK // 2)
    x1, r = _ret_layer(x, a_mod, cos, sin, wts["a_norm_g"], wts["a_w_in"], wts["a_gn_g"],
                       wts["a_w_out"], r0, T=tiles["ret_t"], C=tiles["ret_c"],
                       H=H, DK=DK, DV=DV)
    if past is None:
        f0 = jnp.zeros((B, 1, LANES), F32)
    else:
        k_past, v_past, lf_past = past
        ka_past, va_past, f0 = _past(k_past.astype(F32), v_past.astype(F32),
                                     _pad_lanes(lf_past.astype(F32)), wts["place"],
                                     T=tiles["past_t"], HKV=HKV, DH=DH)
    k, v, lf, ka, va, qa, gb = _prep(
        x1, kv_mod, b_mod, f0, wts["kv_norm_g"], wts["b_norm_g"], wts["kvw"], wts["fw"],
        wts["fb"], wts["k_norm_g"], wts["b_w_in"], wts["b_q_norm_g"], wts["place"],
        T=tiles["attn_tq"], HKV=HKV, HQ=HQ, DH=DH)
    q_off = 0
    if past is not None:
        q_off = k_past.shape[1]
        pad = (-(q_off + L)) % tiles["attn_tk"]
        cat = lambda p, n: jnp.pad(jnp.concatenate([p, n], axis=1), ((0, 0), (0, pad), (0, 0)))
        ka, va = cat(ka_past, ka), cat(va_past, va)
    o = _attn(qa, ka, va, tk=tiles["attn_tk"], HKV=HKV, G=G, DH=DH, q_off=q_off)
    y = _fout(x1, o, gb, b_mod, wts["b_w_out"], T=tiles["attn_tq"])
    return (y, r[None], k, v, lf[:, :, :HQ])


def kernel(x_prompt, x_sample, state_ret, cache_k, cache_v, cache_logf, c_prompt, c_sample, a_norm_g, a_ada_w, a_ada_b, a_w_in, a_gn_g, a_w_out, kv_norm_g, kv_ada_w, kv_ada_b, kv_w, kv_fb, k_norm_g, b_norm_g, b_ada_w, b_ada_b, b_w_in, b_q_norm_g, b_w_out):
    assert a_w_in.shape[0] == 1 and b_w_in.shape[0] == 1, "one retention and one attention layer"
    Bp, Lp, D = x_prompt.shape
    Bs, Ls, _ = x_sample.shape
    _, _, H, DK, DV = state_ret.shape
    _, past_len, HKV, DH = cache_k.shape
    HQ = cache_logf.shape[-1]
    assert DH == LANES and 3 * (HQ // HKV) <= LANES
    dims = (H, DK, DV, HKV, HQ, DH)
    kv_wd = HKV * DH

    c_all = jnp.concatenate([c_prompt, c_sample], axis=0)
    a_mod = _ada(c_all, a_ada_w[0], a_ada_b[0])
    kv_mod = _ada(c_all, kv_ada_w, kv_ada_b)
    b_mod = _ada(c_all, b_ada_w[0], b_ada_b[0])

    wts = dict(
        a_norm_g=a_norm_g[0], a_w_in=a_w_in[0].astype(BF16), a_gn_g=a_gn_g[0],
        a_w_out=a_w_out[0].astype(BF16), kv_norm_g=kv_norm_g,
        kvw=kv_w[:, :2 * kv_wd].astype(BF16), fw=_pad_lanes(kv_w[:, 2 * kv_wd:]).astype(BF16),
        fb=_pad_lanes(kv_fb.reshape(1, HQ)), k_norm_g=k_norm_g, b_norm_g=b_norm_g[0],
        b_w_in=b_w_in[0].astype(BF16), b_q_norm_g=b_q_norm_g[0], b_w_out=b_w_out[0].astype(BF16),
        place=_place_matrix(HKV, HQ // HKV))

    p_tiles = dict(ret_t=min(512, Lp), ret_c=min(256, Lp), attn_tq=min(512, Lp),
                   attn_tk=min(512, Lp))
    yp, rp, kp, vp, lfp = _trunk(x_prompt, a_mod[:Bp], kv_mod[:Bp], b_mod[:Bp], 0, None, None,
                                 wts, dims, p_tiles)
    s_total = past_len + Ls
    s_tiles = dict(ret_t=Ls, ret_c=Ls, attn_tq=Ls, attn_tk=-(-s_total // LANES) * LANES,
                   past_t=min(512, past_len))
    ys, rs, ks, vs, lfs = _trunk(x_sample, a_mod[Bp:], kv_mod[Bp:], b_mod[Bp:], past_len,
                                 state_ret[0], (cache_k, cache_v, cache_logf), wts, dims, s_tiles)
    return (yp, ys, rp, kp, vp, lfp, rs, ks, vs, lfs)
```

```python
import functools
import math

import numpy as np
import jax
import jax.numpy as jnp
from jax import lax
from jax.experimental import pallas as pl
from jax.experimental.pallas import tpu as pltpu

F32 = jnp.float32
BF16 = jnp.bfloat16
EPS = 1e-6
ROPE_BASE = 10000.0
LOG2E = 1.4426950408889634
LANES = 128
VMEM_LIMIT = 56 * 1024 * 1024
M_INIT = -1e30

_NT = (((1,), (1,)), ((), ()))
_TN = (((0,), (0,)), ((), ()))


def _const_spec(shape):
    return pl.BlockSpec(shape, lambda *_: (0,) * len(shape), pipeline_mode=pl.Buffered(1))


def _silu(x):
    return x / (1.0 + jnp.exp(-x))


def _log_sigmoid(x):
    return jnp.minimum(x, 0.0) - jnp.log(1.0 + jnp.exp(-jnp.abs(x)))


def _modulated_norm(x, g, shift, scale):
    xh = x * lax.rsqrt(jnp.mean(x * x, axis=-1, keepdims=True) + EPS)
    return (xh * g) * (1.0 + scale) + shift


def _split3(x):
    hi = x.astype(BF16)
    r1 = x - hi.astype(F32)
    mid = r1.astype(BF16)
    lo = (r1 - mid.astype(F32)).astype(BF16)
    return hi, mid, lo


def _tri_cumsum(lf):
    t = lf.shape[0]
    row = lax.broadcasted_iota(jnp.int32, (t, t), 0)
    col = lax.broadcasted_iota(jnp.int32, (t, t), 1)
    tri = jnp.where(col <= row, 1.0, 0.0).astype(BF16)
    hi, mid, lo = _split3(lf)
    out = jnp.dot(tri, hi, preferred_element_type=F32)
    out = out + jnp.dot(tri, mid, preferred_element_type=F32)
    return out + jnp.dot(tri, lo, preferred_element_type=F32)


def _lane_onehot(rows, lo, hi):
    lane = lax.broadcasted_iota(jnp.int32, (rows, LANES), 1)
    return jnp.where((lane >= lo) & (lane < hi), 1.0, 0.0).astype(BF16)


def _key_bias_cols(fc, place_ref):
    hi, mid, lo = _split3(fc * (-LOG2E))
    cat = jnp.concatenate([hi, mid, lo], axis=-1)
    return jnp.dot(cat, place_ref[...], preferred_element_type=F32).astype(BF16)


def _place_matrix(HKV, G):
    p = np.zeros((3 * LANES, HKV * LANES), np.float32)
    for kvh in range(HKV):
        for g in range(G):
            for j in range(3):
                p[j * LANES + kvh * G + g, kvh * LANES + 3 * g + j] = 1.0
    return jnp.asarray(p, BF16)


def _ada_kernel(c_ref, w_ref, b_ref, o_ref):
    s = _silu(c_ref[...]).astype(BF16)
    o_ref[...] = jnp.dot(s, w_ref[...].astype(BF16), preferred_element_type=F32) + b_ref[...]


def _ada(c, w, b, *, tn=1024):
    m, d = c.shape
    n = w.shape[1]
    return pl.pallas_call(
        _ada_kernel,
        out_shape=jax.ShapeDtypeStruct((m, n), F32),
        grid=(n // tn,),
        in_specs=[pl.BlockSpec((m, d), lambda j: (0, 0)),
                  pl.BlockSpec((d, tn), lambda j: (0, j)),
                  pl.BlockSpec((1, tn), lambda j: (0, j))],
        out_specs=pl.BlockSpec((m, tn), lambda j: (0, j)),
        compiler_params=pltpu.CompilerParams(dimension_semantics=("arbitrary",)),
        name="ada",
    )(c, w, b.reshape(1, n))


def _ret_kernel(*refs, T, C, H, DK, DV, D, has_state):
    if has_state:
        (x_ref, mod_ref, cos_ref, sin_ref, ng_ref, win_ref, gng_ref, wout_ref, r0_ref,
         y_ref, r_ref) = refs
    else:
        (x_ref, mod_ref, cos_ref, sin_ref, ng_ref, win_ref, gng_ref, wout_ref,
         y_ref, r_ref) = refs
        r0_ref = None

    @pl.when(pl.program_id(1) == 0)
    def _():
        if has_state:
            r_ref[...] = r0_ref[...]
        else:
            r_ref[...] = jnp.zeros_like(r_ref)

    x = x_ref[0]
    mod = mod_ref[0]
    shift, scale, gate = mod[:, :D], mod[:, D:2 * D], mod[:, 2 * D:]
    h = _modulated_norm(x, ng_ref[...], shift, scale).astype(BF16)
    cos = cos_ref[...]
    sin = sin_ref[...]
    half = DK // 2
    qk_w = H * DK

    row = lax.broadcasted_iota(jnp.int32, (C, C), 0)
    col = lax.broadcasted_iota(jnp.int32, (C, C), 1)
    diff = row - col
    diff_f = jnp.maximum(diff, 0).astype(F32)
    idx = lax.broadcasted_iota(jnp.int32, (C, 1), 0).astype(F32)

    def rot(t):
        t1, t2 = t[:, :half], t[:, half:]
        return jnp.concatenate([t1 * cos - t2 * sin, t2 * cos + t1 * sin], axis=-1)

    ogs = []
    for hd in range(H):
        lg = math.log1p(-(2.0 ** (-5.0 - hd)))
        q = rot(jnp.dot(h, win_ref[:, hd * DK:(hd + 1) * DK], preferred_element_type=F32))
        k = rot(jnp.dot(h, win_ref[:, qk_w + hd * DK:qk_w + (hd + 1) * DK],
                        preferred_element_type=F32)) * (DK ** -0.5)
        v = jnp.dot(h, win_ref[:, 2 * qk_w + hd * DV:2 * qk_w + (hd + 1) * DV],
                    preferred_element_type=F32).astype(BF16)
        g = jnp.dot(h, win_ref[:, 2 * qk_w + H * DV + hd * DV:2 * qk_w + H * DV + (hd + 1) * DV],
                    preferred_element_type=F32)
        qb = q.astype(BF16)
        kb = k.astype(BF16)
        intra = jnp.where(diff >= 0, jnp.exp(diff_f * lg), 0.0)
        q_dec = jnp.exp((idx + 1.0) * lg)
        k_dec = jnp.exp((C - 1.0 - idx) * lg)
        c_dec = math.exp(C * lg)
        outs = []
        for c in range(T // C):
            sl = slice(c * C, (c + 1) * C)
            r = r_ref[0, hd]
            s = lax.dot_general(qb[sl], kb[sl], _NT, preferred_element_type=F32) * intra
            o = jnp.dot(s.astype(BF16), v[sl], preferred_element_type=F32)
            o = o + jnp.dot(qb[sl], r.astype(BF16), preferred_element_type=F32) * q_dec
            kd = (k[sl] * k_dec).astype(BF16)
            r_ref[0, hd] = r * c_dec + lax.dot_general(kd, v[sl], _TN, preferred_element_type=F32)
            outs.append(o)
        o = outs[0] if len(outs) == 1 else jnp.concatenate(outs, axis=0)
        mu = jnp.mean(o, axis=-1, keepdims=True)
        oc = o - mu
        var = jnp.mean(oc * oc, axis=-1, keepdims=True)
        on = oc * lax.rsqrt(var + EPS) * gng_ref[:, hd * DV:(hd + 1) * DV]
        ogs.append((on * _silu(g)).astype(BF16))
    proj = jnp.dot(jnp.concatenate(ogs, axis=-1), wout_ref[...], preferred_element_type=F32)
    y_ref[0] = x + gate * proj


def _ret_layer(x, mod, cos, sin, norm_g, w_in, gn_g, w_out, r0, *, T, C, H, DK, DV):
    B, L, D = x.shape
    has_state = r0 is not None
    kern = functools.partial(_ret_kernel, T=T, C=C, H=H, DK=DK, DV=DV, D=D, has_state=has_state)
    in_specs = [
        pl.BlockSpec((1, T, D), lambda b, t: (b, t, 0)),
        pl.BlockSpec((1, 1, 3 * D), lambda b, t: (b, 0, 0)),
        pl.BlockSpec((T, DK // 2), lambda b, t: (t, 0)),
        pl.BlockSpec((T, DK // 2), lambda b, t: (t, 0)),
        _const_spec((1, D)),
        _const_spec(w_in.shape),
        _const_spec((1, H * DV)),
        _const_spec(w_out.shape),
    ]
    args = [x, mod.reshape(B, 1, 3 * D), cos, sin, norm_g.reshape(1, D), w_in,
            gn_g.reshape(1, H * DV), w_out]
    if has_state:
        in_specs.append(pl.BlockSpec((1, H, DK, DV), lambda b, t: (b, 0, 0, 0)))
        args.append(r0)
    return pl.pallas_call(
        kern,
        out_shape=(jax.ShapeDtypeStruct((B, L, D), F32),
                   jax.ShapeDtypeStruct((B, H, DK, DV), F32)),
        grid=(B, L // T),
        in_specs=in_specs,
        out_specs=(pl.BlockSpec((1, T, D), lambda b, t: (b, t, 0)),
                   pl.BlockSpec((1, H, DK, DV), lambda b, t: (b, 0, 0, 0))),
        compiler_params=pltpu.CompilerParams(dimension_semantics=("arbitrary", "arbitrary"),
                                             vmem_limit_bytes=VMEM_LIMIT),
        name="ret_layer",
    )(*args)


def _store_kv_wide(ka_ref, va_ref, k_heads, v, fc, place_ref, *, HKV, DH):
    rows = v.shape[0]
    bias = _key_bias_cols(fc, place_ref)
    one = _lane_onehot(rows, 0, 1)
    w = DH + LANES
    for hd in range(HKV):
        ka_ref[0, :, hd * w:hd * w + DH] = k_heads[hd].astype(BF16)
        ka_ref[0, :, hd * w + DH:(hd + 1) * w] = bias[:, hd * LANES:(hd + 1) * LANES]
        va_ref[0, :, hd * w:hd * w + DH] = v[:, hd * DH:(hd + 1) * DH].astype(BF16)
        va_ref[0, :, hd * w + DH:(hd + 1) * w] = one


def _past_kernel(k_ref, v_ref, lf_ref, place_ref, ka_ref, va_ref, fc_ref, carry_ref, *, HKV, DH):
    @pl.when(pl.program_id(1) == 0)
    def _():
        carry_ref[...] = jnp.zeros_like(carry_ref)

    fc = _tri_cumsum(lf_ref[0]) + carry_ref[...]
    carry_ref[...] = fc[-1:, :]
    fc_ref[0] = fc[-1:, :]
    rows = k_ref.shape[1]
    k = k_ref[0].reshape(rows, HKV * DH)
    k_heads = [k[:, hd * DH:(hd + 1) * DH] for hd in range(HKV)]
    _store_kv_wide(ka_ref, va_ref, k_heads, v_ref[0].reshape(rows, HKV * DH), fc, place_ref,
                   HKV=HKV, DH=DH)


def _past(k, v, lf, place, *, T, HKV, DH):
    B, S = k.shape[:2]
    wide = HKV * (DH + LANES)
    tok = lambda w: pl.BlockSpec((1, T, w), lambda b, t: (b, t, 0))
    heads = pl.BlockSpec((1, T, HKV, DH), lambda b, t: (b, t, 0, 0))
    return pl.pallas_call(
        functools.partial(_past_kernel, HKV=HKV, DH=DH),
        out_shape=(jax.ShapeDtypeStruct((B, S, wide), BF16),
                   jax.ShapeDtypeStruct((B, S, wide), BF16),
                   jax.ShapeDtypeStruct((B, 1, LANES), F32)),
        grid=(B, S // T),
        in_specs=[heads, heads, tok(LANES), _const_spec(place.shape)],
        out_specs=(tok(wide), tok(wide), pl.BlockSpec((1, 1, LANES), lambda b, t: (b, 0, 0))),
        scratch_shapes=[pltpu.VMEM((1, LANES), F32)],
        compiler_params=pltpu.CompilerParams(dimension_semantics=("arbitrary", "arbitrary")),
        name="past_kv",
    )(k, v, lf, place)


def _prep_kernel(x_ref, kvmod_ref, bmod_ref, f0_ref, kvng_ref, bng_ref, kvw_ref, fw_ref, fb_ref,
                 kng_ref, bwin_ref, qng_ref, place_ref,
                 k_ref, v_ref, lf_ref, ka_ref, va_ref, qa_ref, gb_ref,
                 carry_ref, *, T, D, HKV, G, DH):
    @pl.when(pl.program_id(1) == 0)
    def _():
        carry_ref[...] = f0_ref[0]

    x = x_ref[0]
    kvmod = kvmod_ref[0]
    hk = _modulated_norm(x, kvng_ref[...], kvmod[:, :D], kvmod[:, D:]).astype(BF16)
    kv_w = HKV * DH
    kng = kng_ref[...]
    k_raw = jnp.dot(hk, kvw_ref[:, :kv_w], preferred_element_type=F32)
    k_heads = []
    for hd in range(HKV):
        kh = k_raw[:, hd * DH:(hd + 1) * DH]
        k_heads.append(kh * lax.rsqrt(jnp.mean(kh * kh, axis=-1, keepdims=True) + EPS) * kng)
    k_ref[0] = jnp.concatenate(k_heads, axis=-1).reshape(T, HKV, DH)
    v = jnp.dot(hk, kvw_ref[:, kv_w:], preferred_element_type=F32)
    v_ref[0] = v.reshape(T, HKV, DH)
    f = jnp.dot(hk, fw_ref[...], preferred_element_type=F32)
    lf = _log_sigmoid(f + fb_ref[...])
    lf_ref[0] = lf
    fc = _tri_cumsum(lf) + carry_ref[...]
    carry_ref[...] = fc[-1:, :]
    _store_kv_wide(ka_ref, va_ref, k_heads, v, fc, place_ref, HKV=HKV, DH=DH)

    bmod = bmod_ref[0]
    hq = _modulated_norm(x, bng_ref[...], bmod[:, :D], bmod[:, D:2 * D]).astype(BF16)
    qng = qng_ref[...] * (DH ** -0.5 * LOG2E)
    fox_w = HKV * G * DH
    q_raw = jnp.dot(hq, bwin_ref[:, :fox_w], preferred_element_type=F32)
    for kvh in range(HKV):
        for g in range(G):
            hd = kvh * G + g
            qh = q_raw[:, hd * DH:(hd + 1) * DH]
            qh = qh * lax.rsqrt(jnp.mean(qh * qh, axis=-1, keepdims=True) + EPS) * qng
            qa_ref[0, 0, kvh, g * T:(g + 1) * T, :DH] = qh.astype(BF16)
            qa_ref[0, 0, kvh, g * T:(g + 1) * T, DH:] = _lane_onehot(T, 3 * g, 3 * g + 3)
    gg = jnp.dot(hq, bwin_ref[:, fox_w:], preferred_element_type=F32)
    gb_ref[0] = _silu(gg).astype(BF16)


def _prep(x1, kvmod, bmod, f0, kv_norm_g, b_norm_g, kvw, fw, fb, k_norm_g, b_w_in, q_norm_g,
          place, *, T, HKV, HQ, DH):
    B, L, D = x1.shape
    G = HQ // HKV
    kv_w = HKV * DH
    fox_w = HQ * DH
    wide = HKV * (DH + LANES)
    kern = functools.partial(_prep_kernel, T=T, D=D, HKV=HKV, G=G, DH=DH)
    tok = lambda w: pl.BlockSpec((1, T, w), lambda b, t: (b, t, 0))
    per_b = lambda w: pl.BlockSpec((1, 1, w), lambda b, t: (b, 0, 0))
    return pl.pallas_call(
        kern,
        out_shape=(jax.ShapeDtypeStruct((B, L, HKV, DH), F32),
                   jax.ShapeDtypeStruct((B, L, HKV, DH), F32),
                   jax.ShapeDtypeStruct((B, L, LANES), F32),
                   jax.ShapeDtypeStruct((B, L, wide), BF16),
                   jax.ShapeDtypeStruct((B, L, wide), BF16),
                   jax.ShapeDtypeStruct((B, L // T, HKV, G * T, DH + LANES), BF16),
                   jax.ShapeDtypeStruct((B, L, fox_w), BF16)),
        grid=(B, L // T),
        in_specs=[tok(D), per_b(2 * D), per_b(3 * D), per_b(LANES),
                  _const_spec((1, D)), _const_spec((1, D)),
                  _const_spec(kvw.shape), _const_spec(fw.shape), _const_spec((1, LANES)),
                  _const_spec((1, DH)), _const_spec(b_w_in.shape), _const_spec((1, DH)),
                  _const_spec(place.shape)],
        out_specs=(pl.BlockSpec((1, T, HKV, DH), lambda b, t: (b, t, 0, 0)),
                   pl.BlockSpec((1, T, HKV, DH), lambda b, t: (b, t, 0, 0)),
                   tok(LANES), tok(wide), tok(wide),
                   pl.BlockSpec((1, 1, HKV, G * T, DH + LANES), lambda b, t: (b, t, 0, 0, 0)),
                   tok(fox_w)),
        scratch_shapes=[pltpu.VMEM((1, LANES), F32)],
        compiler_params=pltpu.CompilerParams(dimension_semantics=("arbitrary", "arbitrary"),
                                             vmem_limit_bytes=VMEM_LIMIT),
        name="prep",
    )(x1, kvmod.reshape(B, 1, 2 * D), bmod.reshape(B, 1, 3 * D), f0, kv_norm_g.reshape(1, D),
      b_norm_g.reshape(1, D), kvw, fw, fb, k_norm_g.reshape(1, DH), b_w_in,
      q_norm_g.reshape(1, DH), place)


_FULL, _DIAG, _SKIP = 0, 1, 2


def _attn_kernel(qi_ref, kj_ref, first_ref, last_ref, var_ref,
                 q_ref, k_ref, v_ref, x_ref, g_ref, mod_ref, wout_ref, y_ref, m_ref, acc_ref,
                 *, variants, tq, ts, HKV, G, DH, D, q_off):
    n = pl.program_id(1)
    w = DH + LANES
    tkb = ts * len(variants[0])

    @pl.when(first_ref[n] == 1)
    def _():
        m_ref[...] = jnp.full_like(m_ref, M_INIT)
        acc_ref[...] = jnp.zeros_like(acc_ref)

    def step(statuses):
        for sub, status in enumerate(statuses):
            if status == _SKIP:
                continue
            rows = slice(sub * ts, (sub + 1) * ts)
            if status == _DIAG:
                qpos = q_off + qi_ref[n] * tq + lax.broadcasted_iota(jnp.int32, (tq, ts), 0)
                kpos = kj_ref[n] * tkb + sub * ts + lax.broadcasted_iota(jnp.int32, (tq, ts), 1)
                keep = jnp.concatenate([kpos <= qpos] * G, axis=0)
            for kvh in range(HKV):
                s = lax.dot_general(q_ref[0, 0, kvh], k_ref[0, rows, kvh * w:(kvh + 1) * w], _NT,
                                    preferred_element_type=F32)
                if status == _DIAG:
                    s = jnp.where(keep, s, -jnp.inf)
                chunks = [s[:, c * LANES:(c + 1) * LANES] for c in range(ts // LANES)]
                mx = functools.reduce(jnp.maximum, chunks)
                m_prev = m_ref[kvh]
                m_new = jnp.maximum(m_prev, jnp.max(mx, axis=-1, keepdims=True))
                alpha = jnp.exp2(m_prev - m_new)
                m_ref[kvh] = m_new
                p = jnp.concatenate([jnp.exp2(c - m_new).astype(BF16) for c in chunks], axis=-1)
                pv = jnp.dot(p, v_ref[0, rows, kvh * w:(kvh + 1) * w], preferred_element_type=F32)
                acc_ref[kvh] = jnp.concatenate([alpha] * (w // LANES), axis=-1) * acc_ref[kvh] + pv

    for vi, statuses in enumerate(variants):
        @pl.when(var_ref[n] == vi)
        def _(statuses=statuses):
            step(statuses)

    @pl.when(last_ref[n] == 1)
    def _():
        ogs = []
        for kvh in range(HKV):
            for g in range(G):
                a = acc_ref[kvh, g * tq:(g + 1) * tq, :]
                hd = kvh * G + g
                o = a[:, :DH] / a[:, DH:DH + 1]
                ogs.append((o * g_ref[0, :, hd * DH:(hd + 1) * DH].astype(F32)).astype(BF16))
        proj = jnp.dot(jnp.concatenate(ogs, axis=-1), wout_ref[...], preferred_element_type=F32)
        y_ref[0] = x_ref[0] + mod_ref[0][:, 2 * D:] * proj


def _attn_tables(L, S, tq, ts, nsub, q_off):
    tkb = ts * nsub
    qi, kj, first, last, var, variants = [], [], [], [], [], []
    for i in range(L // tq):
        q_lo, q_hi = q_off + i * tq, q_off + (i + 1) * tq - 1
        js = [j for j in range(S // tkb) if j * tkb <= q_hi]
        for n, j in enumerate(js):
            statuses = []
            for sub in range(nsub):
                lo = j * tkb + sub * ts
                statuses.append(_SKIP if lo > q_hi else _FULL if lo + ts - 1 <= q_lo else _DIAG)
            statuses = tuple(statuses)
            if statuses not in variants:
                variants.append(statuses)
            qi.append(i)
            kj.append(j)
            first.append(int(n == 0))
            last.append(int(n == len(js) - 1))
            var.append(variants.index(statuses))
    tables = [jnp.asarray(np.asarray(a, np.int32)) for a in (qi, kj, first, last, var)]
    return tuple(variants), tables


def _attn(qa, ka, va, x1, gb, bmod, w_out, *, ts, nsub, HKV, G, DH, q_off):
    B, nq, _, gtq, w = qa.shape
    tq = gtq // G
    L = nq * tq
    D = x1.shape[-1]
    S = ka.shape[1]
    tkb = ts * nsub
    fox_w = HKV * G * DH
    variants, tables = _attn_tables(L, S, tq, ts, nsub, q_off)
    kern = functools.partial(_attn_kernel, variants=variants, tq=tq, ts=ts, HKV=HKV, G=G, DH=DH,
                             D=D, q_off=q_off)
    q_tile = lambda width: pl.BlockSpec((1, tq, width), lambda b, n, qi, kj, *_: (b, qi[n], 0))
    grid_spec = pltpu.PrefetchScalarGridSpec(
        num_scalar_prefetch=5,
        grid=(B, int(tables[0].shape[0])),
        in_specs=[
            pl.BlockSpec((1, 1, HKV, gtq, w), lambda b, n, qi, kj, *_: (b, qi[n], 0, 0, 0)),
            pl.BlockSpec((1, tkb, HKV * w), lambda b, n, qi, kj, *_: (b, kj[n], 0)),
            pl.BlockSpec((1, tkb, HKV * w), lambda b, n, qi, kj, *_: (b, kj[n], 0)),
            q_tile(D), q_tile(fox_w),
            pl.BlockSpec((1, 1, 3 * D), lambda b, n, *_: (b, 0, 0)),
            _const_spec(w_out.shape),
        ],
        out_specs=q_tile(D),
        scratch_shapes=[pltpu.VMEM((HKV, gtq, LANES), F32), pltpu.VMEM((HKV, gtq, w), F32)],
    )
    return pl.pallas_call(
        kern,
        out_shape=jax.ShapeDtypeStruct((B, L, D), F32),
        grid_spec=grid_spec,
        compiler_params=pltpu.CompilerParams(dimension_semantics=("arbitrary", "arbitrary"),
                                             vmem_limit_bytes=VMEM_LIMIT),
        name="fox_attn",
    )(*tables, qa, ka, va, x1, gb, bmod.reshape(B, 1, 3 * D), w_out)


def _rope_tables(pos0, L, half):
    inv = jnp.power(ROPE_BASE, -jnp.arange(half, dtype=F32) / half)
    ang = (pos0 + jnp.arange(L)).astype(F32)[:, None] * inv[None, :]
    return jnp.cos(ang), jnp.sin(ang)


def _pad_lanes(a):
    return jnp.pad(a, [(0, 0)] * (a.ndim - 1) + [(0, LANES - a.shape[-1])])


def _trunk(x, a_mod, kv_mod, b_mod, pos0, r0, past, wts, dims, tiles):
    H, DK, DV, HKV, HQ, DH = dims
    G = HQ // HKV
    B, L, D = x.shape
    cos, sin = _rope_tables(pos0, L, DK // 2)
    x1, r = _ret_layer(x, a_mod, cos, sin, wts["a_norm_g"], wts["a_w_in"], wts["a_gn_g"],
                       wts["a_w_out"], r0, T=tiles["ret_t"], C=tiles["ret_c"],
                       H=H, DK=DK, DV=DV)
    if past is None:
        f0 = jnp.zeros((B, 1, LANES), F32)
    else:
        k_past, v_past, lf_past = past
        ka_past, va_past, f0 = _past(k_past.astype(F32), v_past.astype(F32),
                                     _pad_lanes(lf_past.astype(F32)), wts["place"],
                                     T=tiles["past_t"], HKV=HKV, DH=DH)
    k, v, lf, ka, va, qa, gb = _prep(
        x1, kv_mod, b_mod, f0, wts["kv_norm_g"], wts["b_norm_g"], wts["kvw"], wts["fw"],
        wts["fb"], wts["k_norm_g"], wts["b_w_in"], wts["b_q_norm_g"], wts["place"],
        T=tiles["attn_tq"], HKV=HKV, HQ=HQ, DH=DH)
    q_off = 0
    if past is not None:
        q_off = k_past.shape[1]
        pad = (-(q_off + L)) % (tiles["attn_ts"] * tiles["attn_nsub"])
        cat = lambda p, n: jnp.pad(jnp.concatenate([p, n], axis=1), ((0, 0), (0, pad), (0, 0)))
        ka, va = cat(ka_past, ka), cat(va_past, va)
    y = _attn(qa, ka, va, x1, gb, b_mod, wts["b_w_out"], ts=tiles["attn_ts"],
              nsub=tiles["attn_nsub"], HKV=HKV, G=G, DH=DH, q_off=q_off)
    return (y, r[None], k, v, lf[:, :, :HQ])


def kernel(x_prompt, x_sample, state_ret, cache_k, cache_v, cache_logf, c_prompt, c_sample, a_norm_g, a_ada_w, a_ada_b, a_w_in, a_gn_g, a_w_out, kv_norm_g, kv_ada_w, kv_ada_b, kv_w, kv_fb, k_norm_g, b_norm_g, b_ada_w, b_ada_b, b_w_in, b_q_norm_g, b_w_out):
    assert a_w_in.shape[0] == 1 and b_w_in.shape[0] == 1, "one retention and one attention layer"
    Bp, Lp, D = x_prompt.shape
    Bs, Ls, _ = x_sample.shape
    _, _, H, DK, DV = state_ret.shape
    _, past_len, HKV, DH = cache_k.shape
    HQ = cache_logf.shape[-1]
    assert DH == LANES and 3 * (HQ // HKV) <= LANES
    dims = (H, DK, DV, HKV, HQ, DH)
    kv_wd = HKV * DH

    c_all = jnp.concatenate([c_prompt, c_sample], axis=0)
    a_mod = _ada(c_all, a_ada_w[0], a_ada_b[0])
    kv_mod = _ada(c_all, kv_ada_w, kv_ada_b)
    b_mod = _ada(c_all, b_ada_w[0], b_ada_b[0])

    wts = dict(
        a_norm_g=a_norm_g[0], a_w_in=a_w_in[0].astype(BF16), a_gn_g=a_gn_g[0],
        a_w_out=a_w_out[0].astype(BF16), kv_norm_g=kv_norm_g,
        kvw=kv_w[:, :2 * kv_wd].astype(BF16), fw=_pad_lanes(kv_w[:, 2 * kv_wd:]).astype(BF16),
        fb=_pad_lanes(kv_fb.reshape(1, HQ)), k_norm_g=k_norm_g, b_norm_g=b_norm_g[0],
        b_w_in=b_w_in[0].astype(BF16), b_q_norm_g=b_q_norm_g[0], b_w_out=b_w_out[0].astype(BF16),
        place=_place_matrix(HKV, HQ // HKV))

    p_tiles = dict(ret_t=min(512, Lp), ret_c=min(256, Lp), attn_tq=min(512, Lp),
                   attn_ts=min(512, Lp), attn_nsub=4 if Lp % 2048 == 0 else 1)
    yp, rp, kp, vp, lfp = _trunk(x_prompt, a_mod[:Bp], kv_mod[:Bp], b_mod[:Bp], 0, None, None,
                                 wts, dims, p_tiles)
    s_total = past_len + Ls
    s_tiles = dict(ret_t=Ls, ret_c=Ls, attn_tq=Ls, attn_ts=-(-s_total // LANES) * LANES,
                   attn_nsub=1, past_t=min(512, past_len))
    ys, rs, ks, vs, lfs = _trunk(x_sample, a_mod[Bp:], kv_mod[Bp:], b_mod[Bp:], past_len,
                                 state_ret[0], (cache_k, cache_v, cache_logf), wts, dims, s_tiles)
    return (yp, ys, rp, kp, vp, lfp, rs, ks, vs, lfs)
```

```python
import functools
import math

import numpy as np
import jax
import jax.numpy as jnp
from jax import lax
from jax.experimental import pallas as pl
from jax.experimental.pallas import tpu as pltpu

F32 = jnp.float32
BF16 = jnp.bfloat16
EPS = 1e-6
ROPE_BASE = 10000.0
LOG2E = 1.4426950408889634
LANES = 128
VMEM_LIMIT = 56 * 1024 * 1024
M_INIT = -1e30

_NT = (((1,), (1,)), ((), ()))
_TN = (((0,), (0,)), ((), ()))


def _const_spec(shape):
    return pl.BlockSpec(shape, lambda *_: (0,) * len(shape), pipeline_mode=pl.Buffered(1))


def _silu(x):
    return x / (1.0 + jnp.exp(-x))


def _log_sigmoid(x):
    return jnp.minimum(x, 0.0) - jnp.log(1.0 + jnp.exp(-jnp.abs(x)))


def _modulated_norm(x, g, shift, scale):
    xh = x * lax.rsqrt(jnp.mean(x * x, axis=-1, keepdims=True) + EPS)
    return (xh * g) * (1.0 + scale) + shift


def _split3(x):
    hi = x.astype(BF16)
    r1 = x - hi.astype(F32)
    mid = r1.astype(BF16)
    lo = (r1 - mid.astype(F32)).astype(BF16)
    return hi, mid, lo


def _tri_cumsum(lf):
    t = lf.shape[0]
    row = lax.broadcasted_iota(jnp.int32, (t, t), 0)
    col = lax.broadcasted_iota(jnp.int32, (t, t), 1)
    tri = jnp.where(col <= row, 1.0, 0.0).astype(BF16)
    hi, mid, lo = _split3(lf)
    out = jnp.dot(tri, hi, preferred_element_type=F32)
    out = out + jnp.dot(tri, mid, preferred_element_type=F32)
    return out + jnp.dot(tri, lo, preferred_element_type=F32)


def _lane_onehot(rows, lo, hi):
    lane = lax.broadcasted_iota(jnp.int32, (rows, LANES), 1)
    return jnp.where((lane >= lo) & (lane < hi), 1.0, 0.0).astype(BF16)


def _key_bias_cols(fc, place_ref):
    hi, mid, lo = _split3(fc * (-LOG2E))
    cat = jnp.concatenate([hi, mid, lo], axis=-1)
    return jnp.dot(cat, place_ref[...], preferred_element_type=F32).astype(BF16)


def _place_matrix(HKV, G):
    p = np.zeros((3 * LANES, HKV * LANES), np.float32)
    for kvh in range(HKV):
        for g in range(G):
            for j in range(3):
                p[j * LANES + kvh * G + g, kvh * LANES + 3 * g + j] = 1.0
    return jnp.asarray(p, BF16)


def _ada_kernel(c_ref, w_ref, b_ref, o_ref):
    s = _silu(c_ref[...]).astype(BF16)
    o_ref[...] = jnp.dot(s, w_ref[...].astype(BF16), preferred_element_type=F32) + b_ref[...]


def _ada(c, w, b, *, tn=1024):
    m, d = c.shape
    n = w.shape[1]
    return pl.pallas_call(
        _ada_kernel,
        out_shape=jax.ShapeDtypeStruct((m, n), F32),
        grid=(n // tn,),
        in_specs=[pl.BlockSpec((m, d), lambda j: (0, 0)),
                  pl.BlockSpec((d, tn), lambda j: (0, j)),
                  pl.BlockSpec((1, tn), lambda j: (0, j))],
        out_specs=pl.BlockSpec((m, tn), lambda j: (0, j)),
        compiler_params=pltpu.CompilerParams(dimension_semantics=("arbitrary",)),
        name="ada",
    )(c, w, b.reshape(1, n))


def _ret_kernel(*refs, T, C, H, DK, DV, D, has_state):
    if has_state:
        (x_ref, mod_ref, cos_ref, sin_ref, ng_ref, win_ref, gng_ref, wout_ref, r0_ref,
         y_ref, r_ref) = refs
    else:
        (x_ref, mod_ref, cos_ref, sin_ref, ng_ref, win_ref, gng_ref, wout_ref,
         y_ref, r_ref) = refs
        r0_ref = None

    @pl.when(pl.program_id(1) == 0)
    def _():
        if has_state:
            r_ref[...] = r0_ref[...]
        else:
            r_ref[...] = jnp.zeros_like(r_ref)

    x = x_ref[0]
    mod = mod_ref[0]
    shift, scale, gate = mod[:, :D], mod[:, D:2 * D], mod[:, 2 * D:]
    h = _modulated_norm(x, ng_ref[...], shift, scale).astype(BF16)
    cos = cos_ref[...]
    sin = sin_ref[...]
    half = DK // 2
    qk_w = H * DK

    row = lax.broadcasted_iota(jnp.int32, (C, C), 0)
    col = lax.broadcasted_iota(jnp.int32, (C, C), 1)
    diff = row - col
    diff_f = jnp.maximum(diff, 0).astype(F32)
    idx = lax.broadcasted_iota(jnp.int32, (C, 1), 0).astype(F32)

    def rot(t):
        t1, t2 = t[:, :half], t[:, half:]
        return jnp.concatenate([t1 * cos - t2 * sin, t2 * cos + t1 * sin], axis=-1)

    ogs = []
    for hd in range(H):
        lg = math.log1p(-(2.0 ** (-5.0 - hd)))
        q = rot(jnp.dot(h, win_ref[:, hd * DK:(hd + 1) * DK], preferred_element_type=F32))
        k = rot(jnp.dot(h, win_ref[:, qk_w + hd * DK:qk_w + (hd + 1) * DK],
                        preferred_element_type=F32)) * (DK ** -0.5)
        v = jnp.dot(h, win_ref[:, 2 * qk_w + hd * DV:2 * qk_w + (hd + 1) * DV],
                    preferred_element_type=F32).astype(BF16)
        g = jnp.dot(h, win_ref[:, 2 * qk_w + H * DV + hd * DV:2 * qk_w + H * DV + (hd + 1) * DV],
                    preferred_element_type=F32)
        qb = q.astype(BF16)
        kb = k.astype(BF16)
        intra = jnp.where(diff >= 0, jnp.exp(diff_f * lg), 0.0)
        q_dec = jnp.exp((idx + 1.0) * lg)
        k_dec = jnp.exp((C - 1.0 - idx) * lg)
        c_dec = math.exp(C * lg)
        outs = []
        for c in range(T // C):
            sl = slice(c * C, (c + 1) * C)
            r = r_ref[0, hd]
            s = lax.dot_general(qb[sl], kb[sl], _NT, preferred_element_type=F32) * intra
            o = jnp.dot(s.astype(BF16), v[sl], preferred_element_type=F32)
            o = o + jnp.dot(qb[sl], r.astype(BF16), preferred_element_type=F32) * q_dec
            kd = (k[sl] * k_dec).astype(BF16)
            r_ref[0, hd] = r * c_dec + lax.dot_general(kd, v[sl], _TN, preferred_element_type=F32)
            outs.append(o)
        o = outs[0] if len(outs) == 1 else jnp.concatenate(outs, axis=0)
        mu = jnp.mean(o, axis=-1, keepdims=True)
        oc = o - mu
        var = jnp.mean(oc * oc, axis=-1, keepdims=True)
        on = oc * lax.rsqrt(var + EPS) * gng_ref[:, hd * DV:(hd + 1) * DV]
        ogs.append((on * _silu(g)).astype(BF16))
    proj = jnp.dot(jnp.concatenate(ogs, axis=-1), wout_ref[...], preferred_element_type=F32)
    y_ref[0] = x + gate * proj


def _ret_layer(x, mod, cos, sin, norm_g, w_in, gn_g, w_out, r0, *, T, C, H, DK, DV):
    B, L, D = x.shape
    has_state = r0 is not None
    kern = functools.partial(_ret_kernel, T=T, C=C, H=H, DK=DK, DV=DV, D=D, has_state=has_state)
    in_specs = [
        pl.BlockSpec((1, T, D), lambda b, t: (b, t, 0)),
        pl.BlockSpec((1, 1, 3 * D), lambda b, t: (b, 0, 0)),
        pl.BlockSpec((T, DK // 2), lambda b, t: (t, 0)),
        pl.BlockSpec((T, DK // 2), lambda b, t: (t, 0)),
        _const_spec((1, D)),
        _const_spec(w_in.shape),
        _const_spec((1, H * DV)),
        _const_spec(w_out.shape),
    ]
    args = [x, mod.reshape(B, 1, 3 * D), cos, sin, norm_g.reshape(1, D), w_in,
            gn_g.reshape(1, H * DV), w_out]
    if has_state:
        in_specs.append(pl.BlockSpec((1, H, DK, DV), lambda b, t: (b, 0, 0, 0)))
        args.append(r0)
    return pl.pallas_call(
        kern,
        out_shape=(jax.ShapeDtypeStruct((B, L, D), F32),
                   jax.ShapeDtypeStruct((B, H, DK, DV), F32)),
        grid=(B, L // T),
        in_specs=in_specs,
        out_specs=(pl.BlockSpec((1, T, D), lambda b, t: (b, t, 0)),
                   pl.BlockSpec((1, H, DK, DV), lambda b, t: (b, 0, 0, 0))),
        compiler_params=pltpu.CompilerParams(dimension_semantics=("arbitrary", "arbitrary"),
                                             vmem_limit_bytes=VMEM_LIMIT),
        name="ret_layer",
    )(*args)


def _store_kv_wide(ka_ref, va_ref, k_heads, v, fc, place_ref, *, HKV, DH):
    rows = v.shape[0]
    bias = _key_bias_cols(fc, place_ref)
    one = _lane_onehot(rows, 0, 1)
    w = DH + LANES
    for hd in range(HKV):
        ka_ref[0, :, hd * w:hd * w + DH] = k_heads[hd].astype(BF16)
        ka_ref[0, :, hd * w + DH:(hd + 1) * w] = bias[:, hd * LANES:(hd + 1) * LANES]
        va_ref[0, :, hd * w:hd * w + DH] = v[:, hd * DH:(hd + 1) * DH].astype(BF16)
        va_ref[0, :, hd * w + DH:(hd + 1) * w] = one


def _past_kernel(k_ref, v_ref, lf_ref, place_ref, ka_ref, va_ref, fc_ref, carry_ref, *, HKV, DH):
    @pl.when(pl.program_id(1) == 0)
    def _():
        carry_ref[...] = jnp.zeros_like(carry_ref)

    fc = _tri_cumsum(lf_ref[0]) + carry_ref[...]
    carry_ref[...] = fc[-1:, :]
    fc_ref[0] = fc[-1:, :]
    rows = k_ref.shape[1]
    k = k_ref[0].reshape(rows, HKV * DH)
    k_heads = [k[:, hd * DH:(hd + 1) * DH] for hd in range(HKV)]
    _store_kv_wide(ka_ref, va_ref, k_heads, v_ref[0].reshape(rows, HKV * DH), fc, place_ref,
                   HKV=HKV, DH=DH)


def _past(k, v, lf, place, *, T, HKV, DH):
    B, S = k.shape[:2]
    wide = HKV * (DH + LANES)
    tok = lambda w: pl.BlockSpec((1, T, w), lambda b, t: (b, t, 0))
    heads = pl.BlockSpec((1, T, HKV, DH), lambda b, t: (b, t, 0, 0))
    return pl.pallas_call(
        functools.partial(_past_kernel, HKV=HKV, DH=DH),
        out_shape=(jax.ShapeDtypeStruct((B, S, wide), BF16),
                   jax.ShapeDtypeStruct((B, S, wide), BF16),
                   jax.ShapeDtypeStruct((B, 1, LANES), F32)),
        grid=(B, S // T),
        in_specs=[heads, heads, tok(LANES), _const_spec(place.shape)],
        out_specs=(tok(wide), tok(wide), pl.BlockSpec((1, 1, LANES), lambda b, t: (b, 0, 0))),
        scratch_shapes=[pltpu.VMEM((1, LANES), F32)],
        compiler_params=pltpu.CompilerParams(dimension_semantics=("arbitrary", "arbitrary")),
        name="past_kv",
    )(k, v, lf, place)


def _prep_kernel(x_ref, kvmod_ref, bmod_ref, f0_ref, kvng_ref, bng_ref, kvw_ref, fw_ref, fb_ref,
                 kng_ref, bwin_ref, qng_ref, place_ref,
                 k_ref, v_ref, lf_ref, ka_ref, va_ref, qa_ref, gb_ref,
                 carry_ref, *, T, D, HKV, G, DH):
    @pl.when(pl.program_id(1) == 0)
    def _():
        carry_ref[...] = f0_ref[0]

    x = x_ref[0]
    kvmod = kvmod_ref[0]
    hk = _modulated_norm(x, kvng_ref[...], kvmod[:, :D], kvmod[:, D:]).astype(BF16)
    kv_w = HKV * DH
    kng = kng_ref[...]
    k_raw = jnp.dot(hk, kvw_ref[:, :kv_w], preferred_element_type=F32)
    k_heads = []
    for hd in range(HKV):
        kh = k_raw[:, hd * DH:(hd + 1) * DH]
        k_heads.append(kh * lax.rsqrt(jnp.mean(kh * kh, axis=-1, keepdims=True) + EPS) * kng)
    k_ref[0] = jnp.concatenate(k_heads, axis=-1).reshape(T, HKV, DH)
    v = jnp.dot(hk, kvw_ref[:, kv_w:], preferred_element_type=F32)
    v_ref[0] = v.reshape(T, HKV, DH)
    f = jnp.dot(hk, fw_ref[...], preferred_element_type=F32)
    lf = _log_sigmoid(f + fb_ref[...])
    lf_ref[0] = lf
    fc = _tri_cumsum(lf) + carry_ref[...]
    carry_ref[...] = fc[-1:, :]
    _store_kv_wide(ka_ref, va_ref, k_heads, v, fc, place_ref, HKV=HKV, DH=DH)

    bmod = bmod_ref[0]
    hq = _modulated_norm(x, bng_ref[...], bmod[:, :D], bmod[:, D:2 * D]).astype(BF16)
    qng = qng_ref[...] * (DH ** -0.5 * LOG2E)
    fox_w = HKV * G * DH
    q_raw = jnp.dot(hq, bwin_ref[:, :fox_w], preferred_element_type=F32)
    for kvh in range(HKV):
        for g in range(G):
            hd = kvh * G + g
            qh = q_raw[:, hd * DH:(hd + 1) * DH]
            qh = qh * lax.rsqrt(jnp.mean(qh * qh, axis=-1, keepdims=True) + EPS) * qng
            qa_ref[0, 0, kvh, g * T:(g + 1) * T, :DH] = qh.astype(BF16)
            qa_ref[0, 0, kvh, g * T:(g + 1) * T, DH:] = _lane_onehot(T, 3 * g, 3 * g + 3)
    gg = jnp.dot(hq, bwin_ref[:, fox_w:], preferred_element_type=F32)
    gb_ref[0] = _silu(gg).astype(BF16)


def _prep(x1, kvmod, bmod, f0, kv_norm_g, b_norm_g, kvw, fw, fb, k_norm_g, b_w_in, q_norm_g,
          place, *, T, HKV, HQ, DH):
    B, L, D = x1.shape
    G = HQ // HKV
    kv_w = HKV * DH
    fox_w = HQ * DH
    wide = HKV * (DH + LANES)
    kern = functools.partial(_prep_kernel, T=T, D=D, HKV=HKV, G=G, DH=DH)
    tok = lambda w: pl.BlockSpec((1, T, w), lambda b, t: (b, t, 0))
    per_b = lambda w: pl.BlockSpec((1, 1, w), lambda b, t: (b, 0, 0))
    return pl.pallas_call(
        kern,
        out_shape=(jax.ShapeDtypeStruct((B, L, HKV, DH), F32),
                   jax.ShapeDtypeStruct((B, L, HKV, DH), F32),
                   jax.ShapeDtypeStruct((B, L, LANES), F32),
                   jax.ShapeDtypeStruct((B, L, wide), BF16),
                   jax.ShapeDtypeStruct((B, L, wide), BF16),
                   jax.ShapeDtypeStruct((B, L // T, HKV, G * T, DH + LANES), BF16),
                   jax.ShapeDtypeStruct((B, L, fox_w), BF16)),
        grid=(B, L // T),
        in_specs=[tok(D), per_b(2 * D), per_b(3 * D), per_b(LANES),
                  _const_spec((1, D)), _const_spec((1, D)),
                  _const_spec(kvw.shape), _const_spec(fw.shape), _const_spec((1, LANES)),
                  _const_spec((1, DH)), _const_spec(b_w_in.shape), _const_spec((1, DH)),
                  _const_spec(place.shape)],
        out_specs=(pl.BlockSpec((1, T, HKV, DH), lambda b, t: (b, t, 0, 0)),
                   pl.BlockSpec((1, T, HKV, DH), lambda b, t: (b, t, 0, 0)),
                   tok(LANES), tok(wide), tok(wide),
                   pl.BlockSpec((1, 1, HKV, G * T, DH + LANES), lambda b, t: (b, t, 0, 0, 0)),
                   tok(fox_w)),
        scratch_shapes=[pltpu.VMEM((1, LANES), F32)],
        compiler_params=pltpu.CompilerParams(dimension_semantics=("arbitrary", "arbitrary"),
                                             vmem_limit_bytes=VMEM_LIMIT),
        name="prep",
    )(x1, kvmod.reshape(B, 1, 2 * D), bmod.reshape(B, 1, 3 * D), f0, kv_norm_g.reshape(1, D),
      b_norm_g.reshape(1, D), kvw, fw, fb, k_norm_g.reshape(1, DH), b_w_in,
      q_norm_g.reshape(1, DH), place)


_FULL, _DIAG, _SKIP = 0, 1, 2


def _attn_kernel(qi_ref, kj_ref, first_ref, last_ref, var_ref,
                 q_ref, k_ref, v_ref, x_ref, g_ref, mod_ref, wout_ref, y_ref, m_ref, acc_ref,
                 *, variants, tq, ts, HKV, G, DH, D, q_off):
    n = pl.program_id(1)
    w = DH + LANES
    tkb = ts * len(variants[0])

    @pl.when(first_ref[n] == 1)
    def _():
        m_ref[...] = jnp.full_like(m_ref, M_INIT)
        acc_ref[...] = jnp.zeros_like(acc_ref)

    def step(statuses):
        for sub, status in enumerate(statuses):
            if status == _SKIP:
                continue
            rows = slice(sub * ts, (sub + 1) * ts)
            if status == _DIAG:
                qpos = q_off + qi_ref[n] * tq + lax.broadcasted_iota(jnp.int32, (tq, ts), 0)
                kpos = kj_ref[n] * tkb + sub * ts + lax.broadcasted_iota(jnp.int32, (tq, ts), 1)
                keep = jnp.concatenate([kpos <= qpos] * G, axis=0)
            for kvh in range(HKV):
                s = lax.dot_general(q_ref[0, 0, kvh], k_ref[0, rows, kvh * w:(kvh + 1) * w], _NT,
                                    preferred_element_type=F32)
                if status == _DIAG:
                    s = jnp.where(keep, s, -jnp.inf)
                chunks = [s[:, c * LANES:(c + 1) * LANES] for c in range(ts // LANES)]
                mx = functools.reduce(jnp.maximum, chunks)
                m_prev = m_ref[kvh]
                m_new = jnp.maximum(m_prev, jnp.max(mx, axis=-1, keepdims=True))
                alpha = jnp.exp2(m_prev - m_new)
                m_ref[kvh] = m_new
                p = jnp.concatenate([jnp.exp2(c - m_new).astype(BF16) for c in chunks], axis=-1)
                pv = jnp.dot(p, v_ref[0, rows, kvh * w:(kvh + 1) * w], preferred_element_type=F32)
                acc_ref[kvh] = jnp.concatenate([alpha] * (w // LANES), axis=-1) * acc_ref[kvh] + pv

    for vi, statuses in enumerate(variants):
        @pl.when(var_ref[n] == vi)
        def _(statuses=statuses):
            step(statuses)

    @pl.when(last_ref[n] == 1)
    def _():
        ogs = []
        for kvh in range(HKV):
            for g in range(G):
                a = acc_ref[kvh, g * tq:(g + 1) * tq, :]
                hd = kvh * G + g
                o = a[:, :DH] / a[:, DH:DH + 1]
                ogs.append((o * g_ref[0, :, hd * DH:(hd + 1) * DH].astype(F32)).astype(BF16))
        proj = jnp.dot(jnp.concatenate(ogs, axis=-1), wout_ref[...], preferred_element_type=F32)
        y_ref[0] = x_ref[0] + mod_ref[0][:, 2 * D:] * proj


def _attn_tables(L, S, tq, ts, nsub, q_off):
    tkb = ts * nsub
    qi, kj, first, last, var, variants = [], [], [], [], [], []
    for i in range(L // tq):
        q_lo, q_hi = q_off + i * tq, q_off + (i + 1) * tq - 1
        js = [j for j in range(S // tkb) if j * tkb <= q_hi]
        for n, j in enumerate(js):
            statuses = []
            for sub in range(nsub):
                lo = j * tkb + sub * ts
                statuses.append(_SKIP if lo > q_hi else _FULL if lo + ts - 1 <= q_lo else _DIAG)
            statuses = tuple(statuses)
            if statuses not in variants:
                variants.append(statuses)
            qi.append(i)
            kj.append(j)
            first.append(int(n == 0))
            last.append(int(n == len(js) - 1))
            var.append(variants.index(statuses))
    tables = [jnp.asarray(np.asarray(a, np.int32)) for a in (qi, kj, first, last, var)]
    return tuple(variants), tables


def _attn(qa, ka, va, x1, gb, bmod, w_out, *, ts, nsub, HKV, G, DH, q_off):
    B, nq, _, gtq, w = qa.shape
    tq = gtq // G
    L = nq * tq
    D = x1.shape[-1]
    S = ka.shape[1]
    tkb = ts * nsub
    fox_w = HKV * G * DH
    variants, tables = _attn_tables(L, S, tq, ts, nsub, q_off)
    kern = functools.partial(_attn_kernel, variants=variants, tq=tq, ts=ts, HKV=HKV, G=G, DH=DH,
                             D=D, q_off=q_off)
    q_tile = lambda width: pl.BlockSpec((1, tq, width), lambda b, n, qi, kj, *_: (b, qi[n], 0))
    grid_spec = pltpu.PrefetchScalarGridSpec(
        num_scalar_prefetch=5,
        grid=(B, int(tables[0].shape[0])),
        in_specs=[
            pl.BlockSpec((1, 1, HKV, gtq, w), lambda b, n, qi, kj, *_: (b, qi[n], 0, 0, 0)),
            pl.BlockSpec((1, tkb, HKV * w), lambda b, n, qi, kj, *_: (b, kj[n], 0)),
            pl.BlockSpec((1, tkb, HKV * w), lambda b, n, qi, kj, *_: (b, kj[n], 0)),
            q_tile(D), q_tile(fox_w),
            pl.BlockSpec((1, 1, 3 * D), lambda b, n, *_: (b, 0, 0)),
            _const_spec(w_out.shape),
        ],
        out_specs=q_tile(D),
        scratch_shapes=[pltpu.VMEM((HKV, gtq, LANES), F32), pltpu.VMEM((HKV, gtq, w), F32)],
    )
    return pl.pallas_call(
        kern,
        out_shape=jax.ShapeDtypeStruct((B, L, D), F32),
        grid_spec=grid_spec,
        compiler_params=pltpu.CompilerParams(dimension_semantics=("arbitrary", "arbitrary"),
                                             vmem_limit_bytes=VMEM_LIMIT),
        name="fox_attn",
    )(*tables, qa, ka, va, x1, gb, bmod.reshape(B, 1, 3 * D), w_out)


def _rope_tables(pos0, L, half):
    inv = jnp.power(ROPE_BASE, -jnp.arange(half, dtype=F32) / half)
    ang = (pos0 + jnp.arange(L)).astype(F32)[:, None] * inv[None, :]
    return jnp.cos(ang), jnp.sin(ang)


def _pad_lanes(a):
    return jnp.pad(a, [(0, 0)] * (a.ndim - 1) + [(0, LANES - a.shape[-1])])


def _trunk(x, a_mod, kv_mod, b_mod, pos0, r0, past, wts, dims, tiles):
    H, DK, DV, HKV, HQ, DH = dims
    G = HQ // HKV
    B, L, D = x.shape
    cos, sin = _rope_tables(pos0, L, DK // 2)
    x1, r = _ret_layer(x, a_mod, cos, sin, wts["a_norm_g"], wts["a_w_in"], wts["a_gn_g"],
                       wts["a_w_out"], r0, T=tiles["ret_t"], C=tiles["ret_c"],
                       H=H, DK=DK, DV=DV)
    if past is None:
        f0 = jnp.zeros((B, 1, LANES), F32)
    else:
        k_past, v_past, lf_past = past
        ka_past, va_past, f0 = _past(k_past.astype(F32), v_past.astype(F32),
                                     _pad_lanes(lf_past.astype(F32)), wts["place"],
                                     T=tiles["past_t"], HKV=HKV, DH=DH)
    k, v, lf, ka, va, qa, gb = _prep(
        x1, kv_mod, b_mod, f0, wts["kv_norm_g"], wts["b_norm_g"], wts["kvw"], wts["fw"],
        wts["fb"], wts["k_norm_g"], wts["b_w_in"], wts["b_q_norm_g"], wts["place"],
        T=tiles["attn_tq"], HKV=HKV, HQ=HQ, DH=DH)
    q_off = 0
    if past is not None:
        q_off = k_past.shape[1]
        pad = (-(q_off + L)) % (tiles["attn_ts"] * tiles["attn_nsub"])
        cat = lambda p, n: jnp.pad(jnp.concatenate([p, n], axis=1), ((0, 0), (0, pad), (0, 0)))
        ka, va = cat(ka_past, ka), cat(va_past, va)
    y = _attn(qa, ka, va, x1, gb, b_mod, wts["b_w_out"], ts=tiles["attn_ts"],
              nsub=tiles["attn_nsub"], HKV=HKV, G=G, DH=DH, q_off=q_off)
    return (y, r[None], k, v, lf[:, :, :HQ])


def kernel(x_prompt, x_sample, state_ret, cache_k, cache_v, cache_logf, c_prompt, c_sample, a_norm_g, a_ada_w, a_ada_b, a_w_in, a_gn_g, a_w_out, kv_norm_g, kv_ada_w, kv_ada_b, kv_w, kv_fb, k_norm_g, b_norm_g, b_ada_w, b_ada_b, b_w_in, b_q_norm_g, b_w_out):
    assert a_w_in.shape[0] == 1 and b_w_in.shape[0] == 1, "one retention and one attention layer"
    Bp, Lp, D = x_prompt.shape
    Bs, Ls, _ = x_sample.shape
    _, _, H, DK, DV = state_ret.shape
    _, past_len, HKV, DH = cache_k.shape
    HQ = cache_logf.shape[-1]
    assert DH == LANES and 3 * (HQ // HKV) <= LANES
    dims = (H, DK, DV, HKV, HQ, DH)
    kv_wd = HKV * DH

    c_all = jnp.concatenate([c_prompt, c_sample], axis=0)
    a_mod = _ada(c_all, a_ada_w[0], a_ada_b[0])
    kv_mod = _ada(c_all, kv_ada_w, kv_ada_b)
    b_mod = _ada(c_all, b_ada_w[0], b_ada_b[0])

    wts = dict(
        a_norm_g=a_norm_g[0], a_w_in=a_w_in[0].astype(BF16), a_gn_g=a_gn_g[0],
        a_w_out=a_w_out[0].astype(BF16), kv_norm_g=kv_norm_g,
        kvw=kv_w[:, :2 * kv_wd].astype(BF16), fw=_pad_lanes(kv_w[:, 2 * kv_wd:]).astype(BF16),
        fb=_pad_lanes(kv_fb.reshape(1, HQ)), k_norm_g=k_norm_g, b_norm_g=b_norm_g[0],
        b_w_in=b_w_in[0].astype(BF16), b_q_norm_g=b_q_norm_g[0], b_w_out=b_w_out[0].astype(BF16),
        place=_place_matrix(HKV, HQ // HKV))

    p_tiles = dict(ret_t=min(512, Lp), ret_c=min(256, Lp), attn_tq=min(512, Lp),
                   attn_ts=min(512, Lp), attn_nsub=2 if Lp % 1024 == 0 else 1)
    yp, rp, kp, vp, lfp = _trunk(x_prompt, a_mod[:Bp], kv_mod[:Bp], b_mod[:Bp], 0, None, None,
                                 wts, dims, p_tiles)
    s_total = past_len + Ls
    s_tiles = dict(ret_t=Ls, ret_c=Ls, attn_tq=Ls, attn_ts=-(-s_total // LANES) * LANES,
                   attn_nsub=1, past_t=min(512, past_len))
    ys, rs, ks, vs, lfs = _trunk(x_sample, a_mod[Bp:], kv_mod[Bp:], b_mod[Bp:], past_len,
                                 state_ret[0], (cache_k, cache_v, cache_logf), wts, dims, s_tiles)
    return (yp, ys, rp, kp, vp, lfp, rs, ks, vs, lfs)
```

```python
import functools
import math

import numpy as np
import jax
import jax.numpy as jnp
from jax import lax
from jax.experimental import pallas as pl
from jax.experimental.pallas import tpu as pltpu

F32 = jnp.float32
BF16 = jnp.bfloat16
EPS = 1e-6
ROPE_BASE = 10000.0
LOG2E = 1.4426950408889634
LANES = 128
VMEM_LIMIT = 56 * 1024 * 1024
M_INIT = -1e30

_NT = (((1,), (1,)), ((), ()))
_TN = (((0,), (0,)), ((), ()))


def _const_spec(shape):
    return pl.BlockSpec(shape, lambda *_: (0,) * len(shape), pipeline_mode=pl.Buffered(1))


def _silu(x):
    return x / (1.0 + jnp.exp(-x))


def _log_sigmoid(x):
    return jnp.minimum(x, 0.0) - jnp.log(1.0 + jnp.exp(-jnp.abs(x)))


def _modulated_norm(x, g, shift, scale):
    xh = x * lax.rsqrt(jnp.mean(x * x, axis=-1, keepdims=True) + EPS)
    return (xh * g) * (1.0 + scale) + shift


def _split3(x):
    hi = x.astype(BF16)
    r1 = x - hi.astype(F32)
    mid = r1.astype(BF16)
    lo = (r1 - mid.astype(F32)).astype(BF16)
    return hi, mid, lo


def _tri_cumsum(lf):
    t = lf.shape[0]
    row = lax.broadcasted_iota(jnp.int32, (t, t), 0)
    col = lax.broadcasted_iota(jnp.int32, (t, t), 1)
    tri = jnp.where(col <= row, 1.0, 0.0).astype(BF16)
    hi, mid, lo = _split3(lf)
    out = jnp.dot(tri, hi, preferred_element_type=F32)
    out = out + jnp.dot(tri, mid, preferred_element_type=F32)
    return out + jnp.dot(tri, lo, preferred_element_type=F32)


def _lane_onehot(rows, lo, hi):
    lane = lax.broadcasted_iota(jnp.int32, (rows, LANES), 1)
    return jnp.where((lane >= lo) & (lane < hi), 1.0, 0.0).astype(BF16)


def _key_bias_cols(fc, place_ref):
    hi, mid, lo = _split3(fc * (-LOG2E))
    cat = jnp.concatenate([hi, mid, lo], axis=-1)
    return jnp.dot(cat, place_ref[...], preferred_element_type=F32).astype(BF16)


def _place_matrix(HKV, G):
    p = np.zeros((3 * LANES, HKV * LANES), np.float32)
    for kvh in range(HKV):
        for g in range(G):
            for j in range(3):
                p[j * LANES + kvh * G + g, kvh * LANES + 3 * g + j] = 1.0
    return jnp.asarray(p, BF16)


def _ada_kernel(c_ref, w_ref, b_ref, o_ref):
    s = _silu(c_ref[...]).astype(BF16)
    o_ref[...] = jnp.dot(s, w_ref[...].astype(BF16), preferred_element_type=F32) + b_ref[...]


def _ada(c, w, b, *, tn=1024):
    m, d = c.shape
    n = w.shape[1]
    return pl.pallas_call(
        _ada_kernel,
        out_shape=jax.ShapeDtypeStruct((m, n), F32),
        grid=(n // tn,),
        in_specs=[pl.BlockSpec((m, d), lambda j: (0, 0)),
                  pl.BlockSpec((d, tn), lambda j: (0, j)),
                  pl.BlockSpec((1, tn), lambda j: (0, j))],
        out_specs=pl.BlockSpec((m, tn), lambda j: (0, j)),
        compiler_params=pltpu.CompilerParams(dimension_semantics=("arbitrary",)),
        name="ada",
    )(c, w, b.reshape(1, n))


def _ret_kernel(*refs, T, C, H, DK, DV, D, has_state):
    if has_state:
        (x_ref, mod_ref, cos_ref, sin_ref, ng_ref, win_ref, gng_ref, wout_ref, r0_ref,
         y_ref, r_ref) = refs
    else:
        (x_ref, mod_ref, cos_ref, sin_ref, ng_ref, win_ref, gng_ref, wout_ref,
         y_ref, r_ref) = refs
        r0_ref = None

    @pl.when(pl.program_id(1) == 0)
    def _():
        if has_state:
            r_ref[...] = r0_ref[...]
        else:
            r_ref[...] = jnp.zeros_like(r_ref)

    x = x_ref[0]
    mod = mod_ref[0]
    shift, scale, gate = mod[:, :D], mod[:, D:2 * D], mod[:, 2 * D:]
    h = _modulated_norm(x, ng_ref[...], shift, scale).astype(BF16)
    cos = cos_ref[...]
    sin = sin_ref[...]
    half = DK // 2
    qk_w = H * DK

    row = lax.broadcasted_iota(jnp.int32, (C, C), 0)
    col = lax.broadcasted_iota(jnp.int32, (C, C), 1)
    diff = row - col
    diff_f = jnp.maximum(diff, 0).astype(F32)
    idx = lax.broadcasted_iota(jnp.int32, (C, 1), 0).astype(F32)

    def rot(t):
        t1, t2 = t[:, :half], t[:, half:]
        return jnp.concatenate([t1 * cos - t2 * sin, t2 * cos + t1 * sin], axis=-1)

    ogs = []
    for hd in range(H):
        lg = math.log1p(-(2.0 ** (-5.0 - hd)))
        q = rot(jnp.dot(h, win_ref[:, hd * DK:(hd + 1) * DK], preferred_element_type=F32))
        k = rot(jnp.dot(h, win_ref[:, qk_w + hd * DK:qk_w + (hd + 1) * DK],
                        preferred_element_type=F32)) * (DK ** -0.5)
        v = jnp.dot(h, win_ref[:, 2 * qk_w + hd * DV:2 * qk_w + (hd + 1) * DV],
                    preferred_element_type=F32).astype(BF16)
        g = jnp.dot(h, win_ref[:, 2 * qk_w + H * DV + hd * DV:2 * qk_w + H * DV + (hd + 1) * DV],
                    preferred_element_type=F32)
        qb = q.astype(BF16)
        kb = k.astype(BF16)
        intra = jnp.where(diff >= 0, jnp.exp(diff_f * lg), 0.0)
        q_dec = jnp.exp((idx + 1.0) * lg)
        k_dec = jnp.exp((C - 1.0 - idx) * lg)
        c_dec = math.exp(C * lg)
        outs = []
        for c in range(T // C):
            sl = slice(c * C, (c + 1) * C)
            r = r_ref[0, hd]
            s = lax.dot_general(qb[sl], kb[sl], _NT, preferred_element_type=F32) * intra
            o = jnp.dot(s.astype(BF16), v[sl], preferred_element_type=F32)
            o = o + jnp.dot(qb[sl], r.astype(BF16), preferred_element_type=F32) * q_dec
            kd = (k[sl] * k_dec).astype(BF16)
            r_ref[0, hd] = r * c_dec + lax.dot_general(kd, v[sl], _TN, preferred_element_type=F32)
            outs.append(o)
        o = outs[0] if len(outs) == 1 else jnp.concatenate(outs, axis=0)
        mu = jnp.mean(o, axis=-1, keepdims=True)
        oc = o - mu
        var = jnp.mean(oc * oc, axis=-1, keepdims=True)
        on = oc * lax.rsqrt(var + EPS) * gng_ref[:, hd * DV:(hd + 1) * DV]
        ogs.append((on * _silu(g)).astype(BF16))
    proj = jnp.dot(jnp.concatenate(ogs, axis=-1), wout_ref[...], preferred_element_type=F32)
    y_ref[0] = x + gate * proj


def _ret_layer(x, mod, cos, sin, norm_g, w_in, gn_g, w_out, r0, *, T, C, H, DK, DV):
    B, L, D = x.shape
    has_state = r0 is not None
    kern = functools.partial(_ret_kernel, T=T, C=C, H=H, DK=DK, DV=DV, D=D, has_state=has_state)
    in_specs = [
        pl.BlockSpec((1, T, D), lambda b, t: (b, t, 0)),
        pl.BlockSpec((1, 1, 3 * D), lambda b, t: (b, 0, 0)),
        pl.BlockSpec((T, DK // 2), lambda b, t: (t, 0)),
        pl.BlockSpec((T, DK // 2), lambda b, t: (t, 0)),
        _const_spec((1, D)),
        _const_spec(w_in.shape),
        _const_spec((1, H * DV)),
        _const_spec(w_out.shape),
    ]
    args = [x, mod.reshape(B, 1, 3 * D), cos, sin, norm_g.reshape(1, D), w_in,
            gn_g.reshape(1, H * DV), w_out]
    if has_state:
        in_specs.append(pl.BlockSpec((1, H, DK, DV), lambda b, t: (b, 0, 0, 0)))
        args.append(r0)
    return pl.pallas_call(
        kern,
        out_shape=(jax.ShapeDtypeStruct((B, L, D), F32),
                   jax.ShapeDtypeStruct((B, H, DK, DV), F32)),
        grid=(B, L // T),
        in_specs=in_specs,
        out_specs=(pl.BlockSpec((1, T, D), lambda b, t: (b, t, 0)),
                   pl.BlockSpec((1, H, DK, DV), lambda b, t: (b, 0, 0, 0))),
        compiler_params=pltpu.CompilerParams(dimension_semantics=("arbitrary", "arbitrary"),
                                             vmem_limit_bytes=VMEM_LIMIT),
        name="ret_layer",
    )(*args)


def _store_kv_wide(ka_ref, va_ref, k_heads, v, fc, place_ref, *, HKV, DH):
    rows = v.shape[0]
    bias = _key_bias_cols(fc, place_ref)
    one = _lane_onehot(rows, 0, 1)
    w = DH + LANES
    for hd in range(HKV):
        ka_ref[0, :, hd * w:hd * w + DH] = k_heads[hd].astype(BF16)
        ka_ref[0, :, hd * w + DH:(hd + 1) * w] = bias[:, hd * LANES:(hd + 1) * LANES]
        va_ref[0, :, hd * w:hd * w + DH] = v[:, hd * DH:(hd + 1) * DH].astype(BF16)
        va_ref[0, :, hd * w + DH:(hd + 1) * w] = one


def _past_kernel(k_ref, v_ref, lf_ref, place_ref, ka_ref, va_ref, fc_ref, carry_ref, *, HKV, DH):
    @pl.when(pl.program_id(1) == 0)
    def _():
        carry_ref[...] = jnp.zeros_like(carry_ref)

    fc = _tri_cumsum(lf_ref[0]) + carry_ref[...]
    carry_ref[...] = fc[-1:, :]
    fc_ref[0] = fc[-1:, :]
    rows = k_ref.shape[1]
    k = k_ref[0].reshape(rows, HKV * DH)
    k_heads = [k[:, hd * DH:(hd + 1) * DH] for hd in range(HKV)]
    _store_kv_wide(ka_ref, va_ref, k_heads, v_ref[0].reshape(rows, HKV * DH), fc, place_ref,
                   HKV=HKV, DH=DH)


def _past(k, v, lf, place, *, T, HKV, DH):
    B, S = k.shape[:2]
    wide = HKV * (DH + LANES)
    tok = lambda w: pl.BlockSpec((1, T, w), lambda b, t: (b, t, 0))
    heads = pl.BlockSpec((1, T, HKV, DH), lambda b, t: (b, t, 0, 0))
    return pl.pallas_call(
        functools.partial(_past_kernel, HKV=HKV, DH=DH),
        out_shape=(jax.ShapeDtypeStruct((B, S, wide), BF16),
                   jax.ShapeDtypeStruct((B, S, wide), BF16),
                   jax.ShapeDtypeStruct((B, 1, LANES), F32)),
        grid=(B, S // T),
        in_specs=[heads, heads, tok(LANES), _const_spec(place.shape)],
        out_specs=(tok(wide), tok(wide), pl.BlockSpec((1, 1, LANES), lambda b, t: (b, 0, 0))),
        scratch_shapes=[pltpu.VMEM((1, LANES), F32)],
        compiler_params=pltpu.CompilerParams(dimension_semantics=("arbitrary", "arbitrary")),
        name="past_kv",
    )(k, v, lf, place)


def _prep_kernel(x_ref, kvmod_ref, bmod_ref, f0_ref, kvng_ref, bng_ref, kvw_ref, fw_ref, fb_ref,
                 kng_ref, bwin_ref, qng_ref, place_ref,
                 k_ref, v_ref, lf_ref, ka_ref, va_ref, qa_ref, gb_ref,
                 carry_ref, *, T, D, HKV, G, DH):
    @pl.when(pl.program_id(1) == 0)
    def _():
        carry_ref[...] = f0_ref[0]

    x = x_ref[0]
    kvmod = kvmod_ref[0]
    hk = _modulated_norm(x, kvng_ref[...], kvmod[:, :D], kvmod[:, D:]).astype(BF16)
    kv_w = HKV * DH
    kng = kng_ref[...]
    k_raw = jnp.dot(hk, kvw_ref[:, :kv_w], preferred_element_type=F32)
    k_heads = []
    for hd in range(HKV):
        kh = k_raw[:, hd * DH:(hd + 1) * DH]
        k_heads.append(kh * lax.rsqrt(jnp.mean(kh * kh, axis=-1, keepdims=True) + EPS) * kng)
    k_ref[0] = jnp.concatenate(k_heads, axis=-1).reshape(T, HKV, DH)
    v = jnp.dot(hk, kvw_ref[:, kv_w:], preferred_element_type=F32)
    v_ref[0] = v.reshape(T, HKV, DH)
    f = jnp.dot(hk, fw_ref[...], preferred_element_type=F32)
    lf = _log_sigmoid(f + fb_ref[...])
    lf_ref[0] = lf
    fc = _tri_cumsum(lf) + carry_ref[...]
    carry_ref[...] = fc[-1:, :]
    _store_kv_wide(ka_ref, va_ref, k_heads, v, fc, place_ref, HKV=HKV, DH=DH)

    bmod = bmod_ref[0]
    hq = _modulated_norm(x, bng_ref[...], bmod[:, :D], bmod[:, D:2 * D]).astype(BF16)
    qng = qng_ref[...] * (DH ** -0.5 * LOG2E)
    fox_w = HKV * G * DH
    q_raw = jnp.dot(hq, bwin_ref[:, :fox_w], preferred_element_type=F32)
    for kvh in range(HKV):
        for g in range(G):
            hd = kvh * G + g
            qh = q_raw[:, hd * DH:(hd + 1) * DH]
            qh = qh * lax.rsqrt(jnp.mean(qh * qh, axis=-1, keepdims=True) + EPS) * qng
            qa_ref[0, 0, kvh, g * T:(g + 1) * T, :DH] = qh.astype(BF16)
            qa_ref[0, 0, kvh, g * T:(g + 1) * T, DH:] = _lane_onehot(T, 3 * g, 3 * g + 3)
    gg = jnp.dot(hq, bwin_ref[:, fox_w:], preferred_element_type=F32)
    gb_ref[0] = _silu(gg).astype(BF16)


def _prep(x1, kvmod, bmod, f0, kv_norm_g, b_norm_g, kvw, fw, fb, k_norm_g, b_w_in, q_norm_g,
          place, *, T, HKV, HQ, DH):
    B, L, D = x1.shape
    G = HQ // HKV
    kv_w = HKV * DH
    fox_w = HQ * DH
    wide = HKV * (DH + LANES)
    kern = functools.partial(_prep_kernel, T=T, D=D, HKV=HKV, G=G, DH=DH)
    tok = lambda w: pl.BlockSpec((1, T, w), lambda b, t: (b, t, 0))
    per_b = lambda w: pl.BlockSpec((1, 1, w), lambda b, t: (b, 0, 0))
    return pl.pallas_call(
        kern,
        out_shape=(jax.ShapeDtypeStruct((B, L, HKV, DH), F32),
                   jax.ShapeDtypeStruct((B, L, HKV, DH), F32),
                   jax.ShapeDtypeStruct((B, L, LANES), F32),
                   jax.ShapeDtypeStruct((B, L, wide), BF16),
                   jax.ShapeDtypeStruct((B, L, wide), BF16),
                   jax.ShapeDtypeStruct((B, L // T, HKV, G * T, DH + LANES), BF16),
                   jax.ShapeDtypeStruct((B, L, fox_w), BF16)),
        grid=(B, L // T),
        in_specs=[tok(D), per_b(2 * D), per_b(3 * D), per_b(LANES),
                  _const_spec((1, D)), _const_spec((1, D)),
                  _const_spec(kvw.shape), _const_spec(fw.shape), _const_spec((1, LANES)),
                  _const_spec((1, DH)), _const_spec(b_w_in.shape), _const_spec((1, DH)),
                  _const_spec(place.shape)],
        out_specs=(pl.BlockSpec((1, T, HKV, DH), lambda b, t: (b, t, 0, 0)),
                   pl.BlockSpec((1, T, HKV, DH), lambda b, t: (b, t, 0, 0)),
                   tok(LANES), tok(wide), tok(wide),
                   pl.BlockSpec((1, 1, HKV, G * T, DH + LANES), lambda b, t: (b, t, 0, 0, 0)),
                   tok(fox_w)),
        scratch_shapes=[pltpu.VMEM((1, LANES), F32)],
        compiler_params=pltpu.CompilerParams(dimension_semantics=("arbitrary", "arbitrary"),
                                             vmem_limit_bytes=VMEM_LIMIT),
        name="prep",
    )(x1, kvmod.reshape(B, 1, 2 * D), bmod.reshape(B, 1, 3 * D), f0, kv_norm_g.reshape(1, D),
      b_norm_g.reshape(1, D), kvw, fw, fb, k_norm_g.reshape(1, DH), b_w_in,
      q_norm_g.reshape(1, DH), place)


_FULL, _DIAG, _SKIP = 0, 1, 2


def _attn_kernel(qi_ref, kj_ref, first_ref, last_ref, var_ref,
                 q_ref, k_ref, v_ref, x_ref, g_ref, mod_ref, wout_ref, y_ref, m_ref, acc_ref,
                 *, variants, tq, ts, HKV, G, DH, D, q_off):
    n = pl.program_id(1)
    w = DH + LANES
    tkb = ts * len(variants[0])

    @pl.when(first_ref[n] == 1)
    def _():
        m_ref[...] = jnp.full_like(m_ref, M_INIT)
        acc_ref[...] = jnp.zeros_like(acc_ref)

    def step(statuses):
        for sub, status in enumerate(statuses):
            if status == _SKIP:
                continue
            rows = slice(sub * ts, (sub + 1) * ts)
            if status == _DIAG:
                qpos = q_off + qi_ref[n] * tq + lax.broadcasted_iota(jnp.int32, (tq, ts), 0)
                kpos = kj_ref[n] * tkb + sub * ts + lax.broadcasted_iota(jnp.int32, (tq, ts), 1)
                keep = jnp.concatenate([kpos <= qpos] * G, axis=0)
            for kvh in range(HKV):
                s = lax.dot_general(q_ref[0, 0, kvh], k_ref[0, rows, kvh * w:(kvh + 1) * w], _NT,
                                    preferred_element_type=F32)
                if status == _DIAG:
                    s = jnp.where(keep, s, -jnp.inf)
                chunks = [s[:, c * LANES:(c + 1) * LANES] for c in range(ts // LANES)]
                mx = functools.reduce(jnp.maximum, chunks)
                m_prev = m_ref[kvh]
                m_new = jnp.maximum(m_prev, jnp.max(mx, axis=-1, keepdims=True))
                alpha = jnp.exp2(m_prev - m_new)
                m_ref[kvh] = m_new
                p = jnp.concatenate([jnp.exp2(c - m_new).astype(BF16) for c in chunks], axis=-1)
                pv = jnp.dot(p, v_ref[0, rows, kvh * w:(kvh + 1) * w], preferred_element_type=F32)
                acc_ref[kvh] = jnp.concatenate([alpha] * (w // LANES), axis=-1) * acc_ref[kvh] + pv

    for vi, statuses in enumerate(variants):
        @pl.when(var_ref[n] == vi)
        def _(statuses=statuses):
            step(statuses)

    @pl.when(last_ref[n] == 1)
    def _():
        ogs = []
        for kvh in range(HKV):
            for g in range(G):
                a = acc_ref[kvh, g * tq:(g + 1) * tq, :]
                hd = kvh * G + g
                o = a[:, :DH] / a[:, DH:DH + 1]
                ogs.append((o * g_ref[0, :, hd * DH:(hd + 1) * DH].astype(F32)).astype(BF16))
        proj = jnp.dot(jnp.concatenate(ogs, axis=-1), wout_ref[...], preferred_element_type=F32)
        y_ref[0] = x_ref[0] + mod_ref[0][:, 2 * D:] * proj


def _attn_tables(L, S, tq, ts, nsub, q_off):
    tkb = ts * nsub
    qi, kj, first, last, var, variants = [], [], [], [], [], []
    for i in range(L // tq):
        q_lo, q_hi = q_off + i * tq, q_off + (i + 1) * tq - 1
        js = [j for j in range(-(-S // tkb)) if j * tkb <= q_hi]
        for n, j in enumerate(js):
            statuses = []
            for sub in range(nsub):
                lo = j * tkb + sub * ts
                statuses.append(_SKIP if lo > q_hi else _FULL if lo + ts - 1 <= q_lo else _DIAG)
            statuses = tuple(statuses)
            if statuses not in variants:
                variants.append(statuses)
            qi.append(i)
            kj.append(j)
            first.append(int(n == 0))
            last.append(int(n == len(js) - 1))
            var.append(variants.index(statuses))
    tables = [jnp.asarray(np.asarray(a, np.int32)) for a in (qi, kj, first, last, var)]
    return tuple(variants), tables


def _attn(qa, ka, va, x1, gb, bmod, w_out, *, ts, nsub, HKV, G, DH, q_off):
    B, nq, _, gtq, w = qa.shape
    tq = gtq // G
    L = nq * tq
    D = x1.shape[-1]
    S = ka.shape[1]
    tkb = ts * nsub
    fox_w = HKV * G * DH
    variants, tables = _attn_tables(L, S, tq, ts, nsub, q_off)
    kern = functools.partial(_attn_kernel, variants=variants, tq=tq, ts=ts, HKV=HKV, G=G, DH=DH,
                             D=D, q_off=q_off)
    q_tile = lambda width: pl.BlockSpec((1, tq, width), lambda b, n, qi, kj, *_: (b, qi[n], 0))
    grid_spec = pltpu.PrefetchScalarGridSpec(
        num_scalar_prefetch=5,
        grid=(B, int(tables[0].shape[0])),
        in_specs=[
            pl.BlockSpec((1, 1, HKV, gtq, w), lambda b, n, qi, kj, *_: (b, qi[n], 0, 0, 0)),
            pl.BlockSpec((1, tkb, HKV * w), lambda b, n, qi, kj, *_: (b, kj[n], 0)),
            pl.BlockSpec((1, tkb, HKV * w), lambda b, n, qi, kj, *_: (b, kj[n], 0)),
            q_tile(D), q_tile(fox_w),
            pl.BlockSpec((1, 1, 3 * D), lambda b, n, *_: (b, 0, 0)),
            _const_spec(w_out.shape),
        ],
        out_specs=q_tile(D),
        scratch_shapes=[pltpu.VMEM((HKV, gtq, LANES), F32), pltpu.VMEM((HKV, gtq, w), F32)],
    )
    return pl.pallas_call(
        kern,
        out_shape=jax.ShapeDtypeStruct((B, L, D), F32),
        grid_spec=grid_spec,
        compiler_params=pltpu.CompilerParams(dimension_semantics=("arbitrary", "arbitrary"),
                                             vmem_limit_bytes=VMEM_LIMIT),
        name="fox_attn",
    )(*tables, qa, ka, va, x1, gb, bmod.reshape(B, 1, 3 * D), w_out)


def _rope_tables(pos0, L, half):
    inv = jnp.power(ROPE_BASE, -jnp.arange(half, dtype=F32) / half)
    ang = (pos0 + jnp.arange(L)).astype(F32)[:, None] * inv[None, :]
    return jnp.cos(ang), jnp.sin(ang)


def _pad_lanes(a):
    return jnp.pad(a, [(0, 0)] * (a.ndim - 1) + [(0, LANES - a.shape[-1])])


def _trunk(x, a_mod, kv_mod, b_mod, pos0, r0, past, wts, dims, tiles):
    H, DK, DV, HKV, HQ, DH = dims
    G = HQ // HKV
    B, L, D = x.shape
    cos, sin = _rope_tables(pos0, L, DK // 2)
    x1, r = _ret_layer(x, a_mod, cos, sin, wts["a_norm_g"], wts["a_w_in"], wts["a_gn_g"],
                       wts["a_w_out"], r0, T=tiles["ret_t"], C=tiles["ret_c"],
                       H=H, DK=DK, DV=DV)
    if past is None:
        f0 = jnp.zeros((B, 1, LANES), F32)
    else:
        k_past, v_past, lf_past = past
        ka_past, va_past, f0 = _past(k_past.astype(F32), v_past.astype(F32),
                                     _pad_lanes(lf_past.astype(F32)), wts["place"],
                                     T=tiles["past_t"], HKV=HKV, DH=DH)
    k, v, lf, ka, va, qa, gb = _prep(
        x1, kv_mod, b_mod, f0, wts["kv_norm_g"], wts["b_norm_g"], wts["kvw"], wts["fw"],
        wts["fb"], wts["k_norm_g"], wts["b_w_in"], wts["b_q_norm_g"], wts["place"],
        T=tiles["attn_tq"], HKV=HKV, HQ=HQ, DH=DH)
    q_off = 0
    if past is not None:
        q_off = k_past.shape[1]
        pad = (-(q_off + L)) % (tiles["attn_ts"] * tiles["attn_nsub"])
        cat = lambda p, n: jnp.pad(jnp.concatenate([p, n], axis=1), ((0, 0), (0, pad), (0, 0)))
        ka, va = cat(ka_past, ka), cat(va_past, va)
    y = _attn(qa, ka, va, x1, gb, b_mod, wts["b_w_out"], ts=tiles["attn_ts"],
              nsub=tiles["attn_nsub"], HKV=HKV, G=G, DH=DH, q_off=q_off)
    return (y, r[None], k, v, lf[:, :, :HQ])


def kernel(x_prompt, x_sample, state_ret, cache_k, cache_v, cache_logf, c_prompt, c_sample, a_norm_g, a_ada_w, a_ada_b, a_w_in, a_gn_g, a_w_out, kv_norm_g, kv_ada_w, kv_ada_b, kv_w, kv_fb, k_norm_g, b_norm_g, b_ada_w, b_ada_b, b_w_in, b_q_norm_g, b_w_out):
    assert a_w_in.shape[0] == 1 and b_w_in.shape[0] == 1, "one retention and one attention layer"
    Bp, Lp, D = x_prompt.shape
    Bs, Ls, _ = x_sample.shape
    _, _, H, DK, DV = state_ret.shape
    _, past_len, HKV, DH = cache_k.shape
    HQ = cache_logf.shape[-1]
    assert DH == LANES and 3 * (HQ // HKV) <= LANES
    dims = (H, DK, DV, HKV, HQ, DH)
    kv_wd = HKV * DH

    c_all = jnp.concatenate([c_prompt, c_sample], axis=0)
    a_mod = _ada(c_all, a_ada_w[0], a_ada_b[0])
    kv_mod = _ada(c_all, kv_ada_w, kv_ada_b)
    b_mod = _ada(c_all, b_ada_w[0], b_ada_b[0])

    wts = dict(
        a_norm_g=a_norm_g[0], a_w_in=a_w_in[0].astype(BF16), a_gn_g=a_gn_g[0],
        a_w_out=a_w_out[0].astype(BF16), kv_norm_g=kv_norm_g,
        kvw=kv_w[:, :2 * kv_wd].astype(BF16), fw=_pad_lanes(kv_w[:, 2 * kv_wd:]).astype(BF16),
        fb=_pad_lanes(kv_fb.reshape(1, HQ)), k_norm_g=k_norm_g, b_norm_g=b_norm_g[0],
        b_w_in=b_w_in[0].astype(BF16), b_q_norm_g=b_q_norm_g[0], b_w_out=b_w_out[0].astype(BF16),
        place=_place_matrix(HKV, HQ // HKV))

    p_tiles = dict(ret_t=min(512, Lp), ret_c=min(256, Lp), attn_tq=min(512, Lp),
                   attn_ts=min(512, Lp), attn_nsub=3 if Lp >= 1536 else 1)
    yp, rp, kp, vp, lfp = _trunk(x_prompt, a_mod[:Bp], kv_mod[:Bp], b_mod[:Bp], 0, None, None,
                                 wts, dims, p_tiles)
    s_total = past_len + Ls
    s_tiles = dict(ret_t=Ls, ret_c=Ls, attn_tq=Ls, attn_ts=-(-s_total // LANES) * LANES,
                   attn_nsub=1, past_t=min(512, past_len))
    ys, rs, ks, vs, lfs = _trunk(x_sample, a_mod[Bp:], kv_mod[Bp:], b_mod[Bp:], past_len,
                                 state_ret[0], (cache_k, cache_v, cache_logf), wts, dims, s_tiles)
    return (yp, ys, rp, kp, vp, lfp, rs, ks, vs, lfs)
```

```python
import functools
import math

import numpy as np
import jax
import jax.numpy as jnp
from jax import lax
from jax.experimental import pallas as pl
from jax.experimental.pallas import tpu as pltpu

F32 = jnp.float32
BF16 = jnp.bfloat16
EPS = 1e-6
ROPE_BASE = 10000.0
LOG2E = 1.4426950408889634
LANES = 128
VMEM_LIMIT = 56 * 1024 * 1024
M_INIT = -1e30

_NT = (((1,), (1,)), ((), ()))
_TN = (((0,), (0,)), ((), ()))


def _const_spec(shape):
    return pl.BlockSpec(shape, lambda *_: (0,) * len(shape), pipeline_mode=pl.Buffered(1))


def _silu(x):
    return x / (1.0 + jnp.exp(-x))


def _log_sigmoid(x):
    return jnp.minimum(x, 0.0) - jnp.log(1.0 + jnp.exp(-jnp.abs(x)))


def _modulated_norm(x, g, shift, scale):
    xh = x * lax.rsqrt(jnp.mean(x * x, axis=-1, keepdims=True) + EPS)
    return (xh * g) * (1.0 + scale) + shift


def _split3(x):
    hi = x.astype(BF16)
    r1 = x - hi.astype(F32)
    mid = r1.astype(BF16)
    lo = (r1 - mid.astype(F32)).astype(BF16)
    return hi, mid, lo


def _tri_cumsum(lf):
    t = lf.shape[0]
    row = lax.broadcasted_iota(jnp.int32, (t, t), 0)
    col = lax.broadcasted_iota(jnp.int32, (t, t), 1)
    tri = jnp.where(col <= row, 1.0, 0.0).astype(BF16)
    hi, mid, lo = _split3(lf)
    out = jnp.dot(tri, hi, preferred_element_type=F32)
    out = out + jnp.dot(tri, mid, preferred_element_type=F32)
    return out + jnp.dot(tri, lo, preferred_element_type=F32)


def _lane_onehot(rows, lo, hi):
    lane = lax.broadcasted_iota(jnp.int32, (rows, LANES), 1)
    return jnp.where((lane >= lo) & (lane < hi), 1.0, 0.0).astype(BF16)


def _key_bias_cols(fc, place_ref):
    hi, mid, lo = _split3(fc * (-LOG2E))
    cat = jnp.concatenate([hi, mid, lo], axis=-1)
    return jnp.dot(cat, place_ref[...], preferred_element_type=F32).astype(BF16)


def _place_matrix(HKV, G):
    p = np.zeros((3 * LANES, HKV * LANES), np.float32)
    for kvh in range(HKV):
        for g in range(G):
            for j in range(3):
                p[j * LANES + kvh * G + g, kvh * LANES + 3 * g + j] = 1.0
    return jnp.asarray(p, BF16)


def _ada_kernel(c_ref, w_ref, b_ref, o_ref):
    s = _silu(c_ref[...]).astype(BF16)
    o_ref[...] = jnp.dot(s, w_ref[...].astype(BF16), preferred_element_type=F32) + b_ref[...]


def _ada(c, w, b, *, tn=1024):
    m, d = c.shape
    n = w.shape[1]
    return pl.pallas_call(
        _ada_kernel,
        out_shape=jax.ShapeDtypeStruct((m, n), F32),
        grid=(n // tn,),
        in_specs=[pl.BlockSpec((m, d), lambda j: (0, 0)),
                  pl.BlockSpec((d, tn), lambda j: (0, j)),
                  pl.BlockSpec((1, tn), lambda j: (0, j))],
        out_specs=pl.BlockSpec((m, tn), lambda j: (0, j)),
        compiler_params=pltpu.CompilerParams(dimension_semantics=("arbitrary",)),
        name="ada",
    )(c, w, b.reshape(1, n))


def _ret_kernel(*refs, T, C, H, DK, DV, D, has_state):
    if has_state:
        (x_ref, mod_ref, cos_ref, sin_ref, ng_ref, win_ref, gng_ref, wout_ref, r0_ref,
         y_ref, r_ref) = refs
    else:
        (x_ref, mod_ref, cos_ref, sin_ref, ng_ref, win_ref, gng_ref, wout_ref,
         y_ref, r_ref) = refs
        r0_ref = None

    @pl.when(pl.program_id(1) == 0)
    def _():
        if has_state:
            r_ref[...] = r0_ref[...]
        else:
            r_ref[...] = jnp.zeros_like(r_ref)

    x = x_ref[0]
    mod = mod_ref[0]
    shift, scale, gate = mod[:, :D], mod[:, D:2 * D], mod[:, 2 * D:]
    h = _modulated_norm(x, ng_ref[...], shift, scale).astype(BF16)
    cos = cos_ref[...]
    sin = sin_ref[...]
    half = DK // 2
    qk_w = H * DK

    row = lax.broadcasted_iota(jnp.int32, (C, C), 0)
    col = lax.broadcasted_iota(jnp.int32, (C, C), 1)
    diff = row - col
    diff_f = jnp.maximum(diff, 0).astype(F32)
    idx = lax.broadcasted_iota(jnp.int32, (C, 1), 0).astype(F32)

    def rot(t):
        t1, t2 = t[:, :half], t[:, half:]
        return jnp.concatenate([t1 * cos - t2 * sin, t2 * cos + t1 * sin], axis=-1)

    ogs = []
    for hd in range(H):
        lg = math.log1p(-(2.0 ** (-5.0 - hd)))
        q = rot(jnp.dot(h, win_ref[:, hd * DK:(hd + 1) * DK], preferred_element_type=F32))
        k = rot(jnp.dot(h, win_ref[:, qk_w + hd * DK:qk_w + (hd + 1) * DK],
                        preferred_element_type=F32)) * (DK ** -0.5)
        v = jnp.dot(h, win_ref[:, 2 * qk_w + hd * DV:2 * qk_w + (hd + 1) * DV],
                    preferred_element_type=F32).astype(BF16)
        g = jnp.dot(h, win_ref[:, 2 * qk_w + H * DV + hd * DV:2 * qk_w + H * DV + (hd + 1) * DV],
                    preferred_element_type=F32)
        qb = q.astype(BF16)
        kb = k.astype(BF16)
        intra = jnp.where(diff >= 0, jnp.exp(diff_f * lg), 0.0)
        q_dec = jnp.exp((idx + 1.0) * lg)
        k_dec = jnp.exp((C - 1.0 - idx) * lg)
        c_dec = math.exp(C * lg)
        outs = []
        for c in range(T // C):
            sl = slice(c * C, (c + 1) * C)
            r = r_ref[0, hd]
            s = lax.dot_general(qb[sl], kb[sl], _NT, preferred_element_type=F32) * intra
            o = jnp.dot(s.astype(BF16), v[sl], preferred_element_type=F32)
            o = o + jnp.dot(qb[sl], r.astype(BF16), preferred_element_type=F32) * q_dec
            kd = (k[sl] * k_dec).astype(BF16)
            r_ref[0, hd] = r * c_dec + lax.dot_general(kd, v[sl], _TN, preferred_element_type=F32)
            outs.append(o)
        o = outs[0] if len(outs) == 1 else jnp.concatenate(outs, axis=0)
        mu = jnp.mean(o, axis=-1, keepdims=True)
        oc = o - mu
        var = jnp.mean(oc * oc, axis=-1, keepdims=True)
        on = oc * lax.rsqrt(var + EPS) * gng_ref[:, hd * DV:(hd + 1) * DV]
        ogs.append((on * _silu(g)).astype(BF16))
    proj = jnp.dot(jnp.concatenate(ogs, axis=-1), wout_ref[...], preferred_element_type=F32)
    y_ref[0] = x + gate * proj


def _ret_layer(x, mod, cos, sin, norm_g, w_in, gn_g, w_out, r0, *, T, C, H, DK, DV):
    B, L, D = x.shape
    has_state = r0 is not None
    kern = functools.partial(_ret_kernel, T=T, C=C, H=H, DK=DK, DV=DV, D=D, has_state=has_state)
    in_specs = [
        pl.BlockSpec((1, T, D), lambda b, t: (b, t, 0)),
        pl.BlockSpec((1, 1, 3 * D), lambda b, t: (b, 0, 0)),
        pl.BlockSpec((T, DK // 2), lambda b, t: (t, 0)),
        pl.BlockSpec((T, DK // 2), lambda b, t: (t, 0)),
        _const_spec((1, D)),
        _const_spec(w_in.shape),
        _const_spec((1, H * DV)),
        _const_spec(w_out.shape),
    ]
    args = [x, mod.reshape(B, 1, 3 * D), cos, sin, norm_g.reshape(1, D), w_in,
            gn_g.reshape(1, H * DV), w_out]
    if has_state:
        in_specs.append(pl.BlockSpec((1, H, DK, DV), lambda b, t: (b, 0, 0, 0)))
        args.append(r0)
    return pl.pallas_call(
        kern,
        out_shape=(jax.ShapeDtypeStruct((B, L, D), F32),
                   jax.ShapeDtypeStruct((B, H, DK, DV), F32)),
        grid=(B, L // T),
        in_specs=in_specs,
        out_specs=(pl.BlockSpec((1, T, D), lambda b, t: (b, t, 0)),
                   pl.BlockSpec((1, H, DK, DV), lambda b, t: (b, 0, 0, 0))),
        compiler_params=pltpu.CompilerParams(dimension_semantics=("arbitrary", "arbitrary"),
                                             vmem_limit_bytes=VMEM_LIMIT),
        name="ret_layer",
    )(*args)


def _store_kv_wide(ka_ref, va_ref, k_heads, v, fc, place_ref, *, HKV, DH):
    rows = v.shape[0]
    bias = _key_bias_cols(fc, place_ref)
    one = _lane_onehot(rows, 0, 1)
    w = DH + LANES
    for hd in range(HKV):
        ka_ref[0, :, hd * w:hd * w + DH] = k_heads[hd].astype(BF16)
        ka_ref[0, :, hd * w + DH:(hd + 1) * w] = bias[:, hd * LANES:(hd + 1) * LANES]
        va_ref[0, :, hd * w:hd * w + DH] = v[:, hd * DH:(hd + 1) * DH].astype(BF16)
        va_ref[0, :, hd * w + DH:(hd + 1) * w] = one


def _past_kernel(k_ref, v_ref, lf_ref, place_ref, ka_ref, va_ref, fc_ref, carry_ref, *, HKV, DH):
    @pl.when(pl.program_id(1) == 0)
    def _():
        carry_ref[...] = jnp.zeros_like(carry_ref)

    fc = _tri_cumsum(lf_ref[0]) + carry_ref[...]
    carry_ref[...] = fc[-1:, :]
    fc_ref[0] = fc[-1:, :]
    rows = k_ref.shape[1]
    k = k_ref[0].reshape(rows, HKV * DH)
    k_heads = [k[:, hd * DH:(hd + 1) * DH] for hd in range(HKV)]
    _store_kv_wide(ka_ref, va_ref, k_heads, v_ref[0].reshape(rows, HKV * DH), fc, place_ref,
                   HKV=HKV, DH=DH)


def _past(k, v, lf, place, *, T, HKV, DH):
    B, S = k.shape[:2]
    wide = HKV * (DH + LANES)
    tok = lambda w: pl.BlockSpec((1, T, w), lambda b, t: (b, t, 0))
    heads = pl.BlockSpec((1, T, HKV, DH), lambda b, t: (b, t, 0, 0))
    return pl.pallas_call(
        functools.partial(_past_kernel, HKV=HKV, DH=DH),
        out_shape=(jax.ShapeDtypeStruct((B, S, wide), BF16),
                   jax.ShapeDtypeStruct((B, S, wide), BF16),
                   jax.ShapeDtypeStruct((B, 1, LANES), F32)),
        grid=(B, S // T),
        in_specs=[heads, heads, tok(LANES), _const_spec(place.shape)],
        out_specs=(tok(wide), tok(wide), pl.BlockSpec((1, 1, LANES), lambda b, t: (b, 0, 0))),
        scratch_shapes=[pltpu.VMEM((1, LANES), F32)],
        compiler_params=pltpu.CompilerParams(dimension_semantics=("arbitrary", "arbitrary")),
        name="past_kv",
    )(k, v, lf, place)


def _prep_kernel(x_ref, kvmod_ref, bmod_ref, f0_ref, kvng_ref, bng_ref, kvw_ref, fw_ref, fb_ref,
                 kng_ref, bwin_ref, qng_ref, place_ref,
                 k_ref, v_ref, lf_ref, ka_ref, va_ref, qa_ref, gb_ref,
                 carry_ref, *, T, D, HKV, G, DH):
    @pl.when(pl.program_id(1) == 0)
    def _():
        carry_ref[...] = f0_ref[0]

    x = x_ref[0]
    kvmod = kvmod_ref[0]
    hk = _modulated_norm(x, kvng_ref[...], kvmod[:, :D], kvmod[:, D:]).astype(BF16)
    kv_w = HKV * DH
    kng = kng_ref[...]
    k_raw = jnp.dot(hk, kvw_ref[:, :kv_w], preferred_element_type=F32)
    k_heads = []
    for hd in range(HKV):
        kh = k_raw[:, hd * DH:(hd + 1) * DH]
        k_heads.append(kh * lax.rsqrt(jnp.mean(kh * kh, axis=-1, keepdims=True) + EPS) * kng)
    k_ref[0] = jnp.concatenate(k_heads, axis=-1).reshape(T, HKV, DH)
    v = jnp.dot(hk, kvw_ref[:, kv_w:], preferred_element_type=F32)
    v_ref[0] = v.reshape(T, HKV, DH)
    f = jnp.dot(hk, fw_ref[...], preferred_element_type=F32)
    lf = _log_sigmoid(f + fb_ref[...])
    lf_ref[0] = lf
    fc = _tri_cumsum(lf) + carry_ref[...]
    carry_ref[...] = fc[-1:, :]
    _store_kv_wide(ka_ref, va_ref, k_heads, v, fc, place_ref, HKV=HKV, DH=DH)

    bmod = bmod_ref[0]
    hq = _modulated_norm(x, bng_ref[...], bmod[:, :D], bmod[:, D:2 * D]).astype(BF16)
    qng = qng_ref[...] * (DH ** -0.5 * LOG2E)
    fox_w = HKV * G * DH
    q_raw = jnp.dot(hq, bwin_ref[:, :fox_w], preferred_element_type=F32)
    for kvh in range(HKV):
        for g in range(G):
            hd = kvh * G + g
            qh = q_raw[:, hd * DH:(hd + 1) * DH]
            qh = qh * lax.rsqrt(jnp.mean(qh * qh, axis=-1, keepdims=True) + EPS) * qng
            qa_ref[0, 0, kvh, g * T:(g + 1) * T, :DH] = qh.astype(BF16)
            qa_ref[0, 0, kvh, g * T:(g + 1) * T, DH:] = _lane_onehot(T, 3 * g, 3 * g + 3)
    gg = jnp.dot(hq, bwin_ref[:, fox_w:], preferred_element_type=F32)
    gb_ref[0] = _silu(gg).astype(BF16)


def _prep(x1, kvmod, bmod, f0, kv_norm_g, b_norm_g, kvw, fw, fb, k_norm_g, b_w_in, q_norm_g,
          place, *, T, HKV, HQ, DH):
    B, L, D = x1.shape
    G = HQ // HKV
    kv_w = HKV * DH
    fox_w = HQ * DH
    wide = HKV * (DH + LANES)
    kern = functools.partial(_prep_kernel, T=T, D=D, HKV=HKV, G=G, DH=DH)
    tok = lambda w: pl.BlockSpec((1, T, w), lambda b, t: (b, t, 0))
    per_b = lambda w: pl.BlockSpec((1, 1, w), lambda b, t: (b, 0, 0))
    return pl.pallas_call(
        kern,
        out_shape=(jax.ShapeDtypeStruct((B, L, HKV, DH), F32),
                   jax.ShapeDtypeStruct((B, L, HKV, DH), F32),
                   jax.ShapeDtypeStruct((B, L, LANES), F32),
                   jax.ShapeDtypeStruct((B, L, wide), BF16),
                   jax.ShapeDtypeStruct((B, L, wide), BF16),
                   jax.ShapeDtypeStruct((B, L // T, HKV, G * T, DH + LANES), BF16),
                   jax.ShapeDtypeStruct((B, L, fox_w), BF16)),
        grid=(B, L // T),
        in_specs=[tok(D), per_b(2 * D), per_b(3 * D), per_b(LANES),
                  _const_spec((1, D)), _const_spec((1, D)),
                  _const_spec(kvw.shape), _const_spec(fw.shape), _const_spec((1, LANES)),
                  _const_spec((1, DH)), _const_spec(b_w_in.shape), _const_spec((1, DH)),
                  _const_spec(place.shape)],
        out_specs=(pl.BlockSpec((1, T, HKV, DH), lambda b, t: (b, t, 0, 0)),
                   pl.BlockSpec((1, T, HKV, DH), lambda b, t: (b, t, 0, 0)),
                   tok(LANES), tok(wide), tok(wide),
                   pl.BlockSpec((1, 1, HKV, G * T, DH + LANES), lambda b, t: (b, t, 0, 0, 0)),
                   tok(fox_w)),
        scratch_shapes=[pltpu.VMEM((1, LANES), F32)],
        compiler_params=pltpu.CompilerParams(dimension_semantics=("arbitrary", "arbitrary"),
                                             vmem_limit_bytes=VMEM_LIMIT),
        name="prep",
    )(x1, kvmod.reshape(B, 1, 2 * D), bmod.reshape(B, 1, 3 * D), f0, kv_norm_g.reshape(1, D),
      b_norm_g.reshape(1, D), kvw, fw, fb, k_norm_g.reshape(1, DH), b_w_in,
      q_norm_g.reshape(1, DH), place)


_FULL, _DIAG, _SKIP = 0, 1, 2


def _attn_kernel(qi_ref, kj_ref, first_ref, last_ref, var_ref,
                 q_ref, k_ref, v_ref, x_ref, g_ref, mod_ref, wout_ref, y_ref, m_ref, acc_ref,
                 *, variants, tq, ts, HKV, G, DH, D, q_off):
    n = pl.program_id(1)
    w = DH + LANES
    tkb = ts * len(variants[0])

    @pl.when(n == 0)
    def _():
        m_ref[...] = jnp.full_like(m_ref, M_INIT)
        acc_ref[...] = jnp.zeros_like(acc_ref)

    def step(statuses):
        for sub, status in enumerate(statuses):
            if status == _SKIP:
                continue
            rows = slice(sub * ts, (sub + 1) * ts)
            if status == _DIAG:
                qpos = q_off + qi_ref[n] * tq + lax.broadcasted_iota(jnp.int32, (tq, ts), 0)
                kpos = kj_ref[n] * tkb + sub * ts + lax.broadcasted_iota(jnp.int32, (tq, ts), 1)
                keep = jnp.concatenate([kpos <= qpos] * G, axis=0)
            for kvh in range(HKV):
                s = lax.dot_general(q_ref[0, 0, kvh], k_ref[0, rows, kvh * w:(kvh + 1) * w], _NT,
                                    preferred_element_type=F32)
                if status == _DIAG:
                    s = jnp.where(keep, s, -jnp.inf)
                chunks = [s[:, c * LANES:(c + 1) * LANES] for c in range(ts // LANES)]
                mx = functools.reduce(jnp.maximum, chunks)
                m_prev = m_ref[kvh]
                m_new = jnp.maximum(m_prev, jnp.max(mx, axis=-1, keepdims=True))
                alpha = jnp.exp2(m_prev - m_new)
                m_ref[kvh] = m_new
                p = jnp.concatenate([jnp.exp2(c - m_new).astype(BF16) for c in chunks], axis=-1)
                pv = jnp.dot(p, v_ref[0, rows, kvh * w:(kvh + 1) * w], preferred_element_type=F32)
                acc_ref[kvh] = jnp.concatenate([alpha] * (w // LANES), axis=-1) * acc_ref[kvh] + pv

    for vi, statuses in enumerate(variants):
        @pl.when(var_ref[n] == vi)
        def _(statuses=statuses):
            step(statuses)

    @pl.when(last_ref[n] == 1)
    def _():
        ogs = []
        for kvh in range(HKV):
            for g in range(G):
                a = acc_ref[kvh, g * tq:(g + 1) * tq, :]
                hd = kvh * G + g
                o = a[:, :DH] / a[:, DH:DH + 1]
                ogs.append((o * g_ref[0, :, hd * DH:(hd + 1) * DH].astype(F32)).astype(BF16))
        proj = jnp.dot(jnp.concatenate(ogs, axis=-1), wout_ref[...], preferred_element_type=F32)
        y_ref[0] = x_ref[0] + mod_ref[0][:, 2 * D:] * proj
        m_ref[...] = jnp.full_like(m_ref, M_INIT)
        acc_ref[...] = jnp.zeros_like(acc_ref)


def _attn_tables(L, S, tq, ts, nsub, q_off):
    tkb = ts * nsub
    qi, kj, first, last, var, variants = [], [], [], [], [], []
    for i in range(L // tq):
        q_lo, q_hi = q_off + i * tq, q_off + (i + 1) * tq - 1
        js = [j for j in range(-(-S // tkb)) if j * tkb <= q_hi]
        for n, j in enumerate(js):
            statuses = []
            for sub in range(nsub):
                lo = j * tkb + sub * ts
                statuses.append(_SKIP if lo > q_hi else _FULL if lo + ts - 1 <= q_lo else _DIAG)
            statuses = tuple(statuses)
            if statuses not in variants:
                variants.append(statuses)
            qi.append(i)
            kj.append(j)
            first.append(int(n == 0))
            last.append(int(n == len(js) - 1))
            var.append(variants.index(statuses))
    tables = [jnp.asarray(np.asarray(a, np.int32)) for a in (qi, kj, first, last, var)]
    return tuple(variants), tables


def _attn(qa, ka, va, x1, gb, bmod, w_out, *, ts, nsub, HKV, G, DH, q_off):
    B, nq, _, gtq, w = qa.shape
    tq = gtq // G
    L = nq * tq
    D = x1.shape[-1]
    S = ka.shape[1]
    tkb = ts * nsub
    fox_w = HKV * G * DH
    variants, tables = _attn_tables(L, S, tq, ts, nsub, q_off)
    kern = functools.partial(_attn_kernel, variants=variants, tq=tq, ts=ts, HKV=HKV, G=G, DH=DH,
                             D=D, q_off=q_off)
    q_tile = lambda width: pl.BlockSpec((1, tq, width), lambda b, n, qi, kj, *_: (b, qi[n], 0))
    grid_spec = pltpu.PrefetchScalarGridSpec(
        num_scalar_prefetch=5,
        grid=(B, int(tables[0].shape[0])),
        in_specs=[
            pl.BlockSpec((1, 1, HKV, gtq, w), lambda b, n, qi, kj, *_: (b, qi[n], 0, 0, 0)),
            pl.BlockSpec((1, tkb, HKV * w), lambda b, n, qi, kj, *_: (b, kj[n], 0)),
            pl.BlockSpec((1, tkb, HKV * w), lambda b, n, qi, kj, *_: (b, kj[n], 0)),
            q_tile(D), q_tile(fox_w),
            pl.BlockSpec((1, 1, 3 * D), lambda b, n, *_: (b, 0, 0)),
            _const_spec(w_out.shape),
        ],
        out_specs=q_tile(D),
        scratch_shapes=[pltpu.VMEM((HKV, gtq, LANES), F32), pltpu.VMEM((HKV, gtq, w), F32)],
    )
    return pl.pallas_call(
        kern,
        out_shape=jax.ShapeDtypeStruct((B, L, D), F32),
        grid_spec=grid_spec,
        compiler_params=pltpu.CompilerParams(dimension_semantics=("arbitrary", "arbitrary"),
                                             vmem_limit_bytes=VMEM_LIMIT),
        name="fox_attn",
    )(*tables, qa, ka, va, x1, gb, bmod.reshape(B, 1, 3 * D), w_out)


def _rope_tables(pos0, L, half):
    inv = jnp.power(ROPE_BASE, -jnp.arange(half, dtype=F32) / half)
    ang = (pos0 + jnp.arange(L)).astype(F32)[:, None] * inv[None, :]
    return jnp.cos(ang), jnp.sin(ang)


def _pad_lanes(a):
    return jnp.pad(a, [(0, 0)] * (a.ndim - 1) + [(0, LANES - a.shape[-1])])


def _trunk(x, a_mod, kv_mod, b_mod, pos0, r0, past, wts, dims, tiles):
    H, DK, DV, HKV, HQ, DH = dims
    G = HQ // HKV
    B, L, D = x.shape
    cos, sin = _rope_tables(pos0, L, DK // 2)
    x1, r = _ret_layer(x, a_mod, cos, sin, wts["a_norm_g"], wts["a_w_in"], wts["a_gn_g"],
                       wts["a_w_out"], r0, T=tiles["ret_t"], C=tiles["ret_c"],
                       H=H, DK=DK, DV=DV)
    if past is None:
        f0 = jnp.zeros((B, 1, LANES), F32)
    else:
        k_past, v_past, lf_past = past
        ka_past, va_past, f0 = _past(k_past.astype(F32), v_past.astype(F32),
                                     _pad_lanes(lf_past.astype(F32)), wts["place"],
                                     T=tiles["past_t"], HKV=HKV, DH=DH)
    k, v, lf, ka, va, qa, gb = _prep(
        x1, kv_mod, b_mod, f0, wts["kv_norm_g"], wts["b_norm_g"], wts["kvw"], wts["fw"],
        wts["fb"], wts["k_norm_g"], wts["b_w_in"], wts["b_q_norm_g"], wts["place"],
        T=tiles["attn_tq"], HKV=HKV, HQ=HQ, DH=DH)
    q_off = 0
    if past is not None:
        q_off = k_past.shape[1]
        pad = (-(q_off + L)) % (tiles["attn_ts"] * tiles["attn_nsub"])
        cat = lambda p, n: jnp.pad(jnp.concatenate([p, n], axis=1), ((0, 0), (0, pad), (0, 0)))
        ka, va = cat(ka_past, ka), cat(va_past, va)
    y = _attn(qa, ka, va, x1, gb, b_mod, wts["b_w_out"], ts=tiles["attn_ts"],
              nsub=tiles["attn_nsub"], HKV=HKV, G=G, DH=DH, q_off=q_off)
    return (y, r[None], k, v, lf[:, :, :HQ])


def kernel(x_prompt, x_sample, state_ret, cache_k, cache_v, cache_logf, c_prompt, c_sample, a_norm_g, a_ada_w, a_ada_b, a_w_in, a_gn_g, a_w_out, kv_norm_g, kv_ada_w, kv_ada_b, kv_w, kv_fb, k_norm_g, b_norm_g, b_ada_w, b_ada_b, b_w_in, b_q_norm_g, b_w_out):
    assert a_w_in.shape[0] == 1 and b_w_in.shape[0] == 1, "one retention and one attention layer"
    Bp, Lp, D = x_prompt.shape
    Bs, Ls, _ = x_sample.shape
    _, _, H, DK, DV = state_ret.shape
    _, past_len, HKV, DH = cache_k.shape
    HQ = cache_logf.shape[-1]
    assert DH == LANES and 3 * (HQ // HKV) <= LANES
    dims = (H, DK, DV, HKV, HQ, DH)
    kv_wd = HKV * DH

    c_all = jnp.concatenate([c_prompt, c_sample], axis=0)
    a_mod = _ada(c_all, a_ada_w[0], a_ada_b[0])
    kv_mod = _ada(c_all, kv_ada_w, kv_ada_b)
    b_mod = _ada(c_all, b_ada_w[0], b_ada_b[0])

    wts = dict(
        a_norm_g=a_norm_g[0], a_w_in=a_w_in[0].astype(BF16), a_gn_g=a_gn_g[0],
        a_w_out=a_w_out[0].astype(BF16), kv_norm_g=kv_norm_g,
        kvw=kv_w[:, :2 * kv_wd].astype(BF16), fw=_pad_lanes(kv_w[:, 2 * kv_wd:]).astype(BF16),
        fb=_pad_lanes(kv_fb.reshape(1, HQ)), k_norm_g=k_norm_g, b_norm_g=b_norm_g[0],
        b_w_in=b_w_in[0].astype(BF16), b_q_norm_g=b_q_norm_g[0], b_w_out=b_w_out[0].astype(BF16),
        place=_place_matrix(HKV, HQ // HKV))

    p_tiles = dict(ret_t=min(512, Lp), ret_c=min(256, Lp), attn_tq=min(512, Lp),
                   attn_ts=min(512, Lp), attn_nsub=3 if Lp >= 1536 else 1)
    yp, rp, kp, vp, lfp = _trunk(x_prompt, a_mod[:Bp], kv_mod[:Bp], b_mod[:Bp], 0, None, None,
                                 wts, dims, p_tiles)
    s_total = past_len + Ls
    s_tiles = dict(ret_t=Ls, ret_c=Ls, attn_tq=Ls, attn_ts=-(-s_total // LANES) * LANES,
                   attn_nsub=1, past_t=min(512, past_len))
    ys, rs, ks, vs, lfs = _trunk(x_sample, a_mod[Bp:], kv_mod[Bp:], b_mod[Bp:], past_len,
                                 state_ret[0], (cache_k, cache_v, cache_logf), wts, dims, s_tiles)
    return (yp, ys, rp, kp, vp, lfp, rs, ks, vs, lfs)
```
